```python
import jax, jax.numpy as jnp
from jax import lax
import numpy as np

D_MODEL = 1024
BATCH = 16
SEQ = 4096
DEPTH = 2

CHUNK = 64
Q_BLOCK = 128
FOX_HEADS = 6
FOX_HEAD_DIM = 64
FOX_WIDTH = FOX_HEADS * FOX_HEAD_DIM
FOX_FORGET_BIAS_MEAN = 3.0
HGRN_HEADS = 4
HGRN_KEY_DIM = 128
HGRN_VAL_DIM = 96
HGRN_KEY_WIDTH = HGRN_HEADS * HGRN_KEY_DIM
HGRN_VAL_WIDTH = HGRN_HEADS * HGRN_VAL_DIM
CONV_CH = 256
CONV_WIDTH = 31
N_BRANCH = 3
BRANCH_WIDTH = FOX_WIDTH + HGRN_VAL_WIDTH + CONV_CH
D_FF = 2816
FFN_CONV_WIDTH = 3
EPS = 1e-6

IN_SIZES = (FOX_WIDTH, FOX_WIDTH, FOX_WIDTH, FOX_HEADS,
            HGRN_KEY_WIDTH, HGRN_KEY_WIDTH, HGRN_VAL_WIDTH, HGRN_VAL_WIDTH,
            CONV_CH, CONV_CH, N_BRANCH * D_MODEL)
IN_WIDTH = sum(IN_SIZES)
IN_SPLITS = tuple(int(v) for v in np.cumsum(IN_SIZES)[:-1])

kernel_name = "hybrid_fox_hgrn2_conformer_gated_trunk"


def rmsnorm(x, g):
    x32 = x.astype(jnp.float32)
    y = x32 * lax.rsqrt(jnp.mean(x32 * x32, axis=-1, keepdims=True) + EPS)
    return (y * g.astype(jnp.float32)).astype(x.dtype)


def layernorm(x, g, b):
    x32 = x.astype(jnp.float32)
    mu = jnp.mean(x32, axis=-1, keepdims=True)
    xc = x32 - mu
    y = xc * lax.rsqrt(jnp.mean(xc * xc, axis=-1, keepdims=True) + EPS)
    return (y * g.astype(jnp.float32) + b.astype(jnp.float32)).astype(x.dtype)


def causal_dwconv(u, w, b):
    k = w.shape[0]
    y = lax.conv_general_dilated(
        u, w[:, None, :].astype(u.dtype), window_strides=(1,), padding=[(k - 1, 0)],
        dimension_numbers=("NWC", "WIO", "NWC"), feature_group_count=u.shape[-1])
    return y + b.astype(u.dtype)


def split_heads(t, n_heads):
    b, s, w = t.shape
    return t.reshape(b, s, n_heads, w // n_heads).transpose(0, 2, 1, 3)


def fox_attention(q, k, v, logf):
    s_len = q.shape[2]
    scale = q.shape[-1] ** -0.5
    c = jnp.cumsum(logf, axis=-1)
    outs = []
    for start in range(0, s_len, Q_BLOCK):
        end = start + Q_BLOCK
        qb = q[:, :, start:end]
        kb = k[:, :, :end]
        vb = v[:, :, :end]
        logits = jnp.einsum("bhqd,bhkd->bhqk", qb, kb).astype(jnp.float32) * scale
        logits = logits + (c[:, :, start:end, None] - c[:, :, None, :end])
        mask = (start + jnp.arange(Q_BLOCK))[:, None] >= jnp.arange(end)[None, :]
        logits = jnp.where(mask, logits, -jnp.inf)
        p = jax.nn.softmax(logits, axis=-1)
        outs.append(jnp.einsum("bhqk,bhkd->bhqd", p.astype(v.dtype), vb))
    return jnp.concatenate(outs, axis=2)


def hgrn2_chunkwise(q, k, v, logf):
    b, h, s_len, kd = q.shape
    vd = v.shape[-1]
    n = s_len // CHUNK
    def to_chunks(t):
        t = t.astype(jnp.float32).reshape(b, h, n, CHUNK, t.shape[-1])
        return jnp.moveaxis(t, 2, 0)
    qc, kc, vc, lc = to_chunks(q), to_chunks(k), to_chunks(v), to_chunks(logf)
    bc = jnp.cumsum(lc, axis=-2)
    tri = jnp.tril(jnp.ones((CHUNK, CHUNK), dtype=bool))

    def step(state, inp):
        qx, kx, vx, bx = inp
        diff = bx[:, :, :, None, :] - bx[:, :, None, :, :]
        decay = jnp.exp(jnp.where(tri[:, :, None], diff, -jnp.inf))
        attn = jnp.einsum("bhtk,bhsk,bhtsk->bhts", qx, kx, decay)
        o = (jnp.einsum("bhts,bhsv->bhtv", attn, vx)
             + jnp.einsum("bhtk,bhkv->bhtv", qx * jnp.exp(bx), state))
        b_last = bx[:, :, -1:, :]
        new_state = (jnp.exp(b_last[:, :, 0, :])[..., None] * state
                     + jnp.einsum("bhsk,bhsv->bhkv", kx * jnp.exp(b_last - bx), vx))
        return new_state, o

    state0 = jnp.zeros((b, h, kd, vd), jnp.float32)
    _, ys = lax.scan(step, state0, (qc, kc, vc, bc))
    ys = jnp.moveaxis(ys, 0, 2).reshape(b, h, s_len, vd)
    return ys.astype(v.dtype)


def setup_inputs(seed: int = 0) -> dict:
    key = jax.random.key(seed)
    ks = jax.random.split(key, 21)
    L, D = DEPTH, D_MODEL
    def nrm(k, shape, scale):
        return jax.random.normal(k, shape, jnp.float32) * scale
    return {
        "x": nrm(ks[0], (BATCH, SEQ, D), 1.0),
        "norm_mix_g": 1.0 + nrm(ks[1], (L, D), 0.02),
        "w_in": nrm(ks[2], (L, D, IN_WIDTH), D ** -0.5),
        "fox_forget_b": FOX_FORGET_BIAS_MEAN + nrm(ks[3], (L, FOX_HEADS), 0.1),
        "fox_q_norm_g": 1.0 + nrm(ks[4], (L, FOX_HEAD_DIM), 0.02),
        "fox_k_norm_g": 1.0 + nrm(ks[5], (L, FOX_HEAD_DIM), 0.02),
        "hgrn_lb_logits": nrm(ks[6], (L, HGRN_KEY_WIDTH), 0.5),
        "hgrn_out_norm_g": 1.0 + nrm(ks[7], (L, HGRN_VAL_DIM), 0.02),
        "conv_dw_w": nrm(ks[8], (L, CONV_WIDTH, CONV_CH), CONV_WIDTH ** -0.5),
        "conv_dw_b": nrm(ks[9], (L, CONV_CH), 0.02),
        "conv_norm_g": 1.0 + nrm(ks[10], (L, CONV_CH), 0.02),
        "conv_norm_b": nrm(ks[11], (L, CONV_CH), 0.02),
        "gate_b": nrm(ks[12], (L, N_BRANCH * D), 0.1),
        "w_branch": nrm(ks[13], (L, BRANCH_WIDTH, D), FOX_WIDTH ** -0.5),
        "w_out": nrm(ks[14], (L, D, D), D ** -0.5),
        "norm_ffn_g": 1.0 + nrm(ks[15], (L, D), 0.02),
        "w_up": nrm(ks[16], (L, D, 2 * D_FF), D ** -0.5),
        "ffn_dw_w": nrm(ks[17], (L, FFN_CONV_WIDTH, D_FF), FFN_CONV_WIDTH ** -0.5),
        "ffn_dw_b": nrm(ks[18], (L, D_FF), 0.02),
        "w_down": nrm(ks[19], (L, D_FF, D), D_FF ** -0.5),
    }


def reference(x, norm_mix_g, w_in, fox_forget_b, fox_q_norm_g, fox_k_norm_g,
              hgrn_lb_logits, hgrn_out_norm_g, conv_dw_w, conv_dw_b, conv_norm_g,
              conv_norm_b, gate_b, w_branch, w_out, norm_ffn_g, w_up, ffn_dw_w,
              ffn_dw_b, w_down):
    b, s_len, d = x.shape
    lb_all = jnp.cumsum(jax.nn.softmax(hgrn_lb_logits.astype(jnp.float32), axis=0), axis=0)
    lb_all = jnp.maximum(lb_all - lb_all[0:1], 0.0)
    o_a = FOX_WIDTH
    o_b = FOX_WIDTH + HGRN_VAL_WIDTH
    for l in range(DEPTH):
        h = rmsnorm(x, norm_mix_g[l])
        z = h @ w_in[l]
        (fq, fk, fv, ff, hq, hf, hi, hg, ca, cb, gl) = jnp.split(z, IN_SPLITS, axis=-1)

        qa = rmsnorm(split_heads(fq, FOX_HEADS), fox_q_norm_g[l])
        ka = rmsnorm(split_heads(fk, FOX_HEADS), fox_k_norm_g[l])
        va = split_heads(fv, FOX_HEADS)
        logf_a = jax.nn.log_sigmoid(ff.astype(jnp.float32) + fox_forget_b[l]).transpose(0, 2, 1)
        ya = fox_attention(qa, ka, va, logf_a)
        ya = ya.transpose(0, 2, 1, 3).reshape(b, s_len, FOX_WIDTH)

        lb = lb_all[l]
        logf_b = jnp.logaddexp(jnp.log(lb), jnp.log1p(-lb) + jax.nn.log_sigmoid(hf.astype(jnp.float32)))
        kb_in = -jnp.expm1(logf_b)
        yb = hgrn2_chunkwise(split_heads(hq, HGRN_HEADS),
                             split_heads(kb_in.astype(x.dtype), HGRN_HEADS),
                             split_heads(hi, HGRN_HEADS),
                             split_heads(logf_b, HGRN_HEADS))
        yb = rmsnorm(yb.transpose(0, 2, 1, 3), hgrn_out_norm_g[l])
        yb = yb * jax.nn.silu(hg.reshape(b, s_len, HGRN_HEADS, HGRN_VAL_DIM))
        yb = yb.reshape(b, s_len, HGRN_VAL_WIDTH)

        yc = ca * jax.nn.sigmoid(cb)
        yc = causal_dwconv(yc, conv_dw_w[l], conv_dw_b[l])
        yc = jax.nn.silu(layernorm(yc, conv_norm_g[l], conv_norm_b[l]))

        gates = jax.nn.sigmoid((gl + gate_b[l]).astype(jnp.float32)).astype(x.dtype)
        gates = gates.reshape(b, s_len, N_BRANCH, d)
        wb = w_branch[l]
        merged = (gates[:, :, 0] * (ya @ wb[:o_a])
                  + gates[:, :, 1] * (yb @ wb[o_a:o_b])
                  + gates[:, :, 2] * (yc @ wb[o_b:]))
        x = x + merged @ w_out[l]

        h2 = rmsnorm(x, norm_ffn_g[l])
        gate_in, value_in = jnp.split(h2 @ w_up[l], 2, axis=-1)
        gate_in = causal_dwconv(gate_in, ffn_dw_w[l], ffn_dw_b[l])
        x = x + (jax.nn.silu(gate_in) * value_in) @ w_down[l]
    return x
```

```python
import functools

import numpy as np
import jax
import jax.numpy as jnp
from jax import lax
from jax.experimental import pallas as pl
from jax.experimental.pallas import tpu as pltpu

D_MODEL = 1024
CHUNK = 64
FOX_HEADS = 6
FOX_HEAD_DIM = 64
FOX_WIDTH = FOX_HEADS * FOX_HEAD_DIM
HGRN_HEADS = 4
HGRN_KEY_DIM = 128
HGRN_VAL_DIM = 96
HGRN_KEY_WIDTH = HGRN_HEADS * HGRN_KEY_DIM
HGRN_VAL_WIDTH = HGRN_HEADS * HGRN_VAL_DIM
CONV_CH = 256
CONV_WIDTH = 31
N_BRANCH = 3
D_FF = 2816
FFN_CONV_WIDTH = 3
EPS = 1e-6

LANE = 128
SUBLANE = 8
VMEM_LIMIT_BYTES = 56 * 1024 * 1024

HGRN_VAL_PAD = LANE
HGRN_PAD_WIDTH = HGRN_HEADS * HGRN_VAL_PAD
FOX_AUG = LANE
FOX_VT_ROWS = 80
FOX_PAIRS = FOX_HEADS // 2
NEG_BIG = -1e30

QKV_W = 3 * FOX_WIDTH
FF_W = LANE
HQIG_W = HGRN_KEY_WIDTH + 2 * HGRN_PAD_WIDTH
HF_W = HGRN_KEY_WIDTH
CAB_W = 2 * CONV_CH
GL_W = N_BRANCH * D_MODEL
IN_PROJ_W = QKV_W + FF_W + HQIG_W + HF_W + CAB_W + GL_W

f32 = jnp.float32
bf16 = jnp.bfloat16


def _cparams(semantics):
    return pltpu.CompilerParams(dimension_semantics=semantics, vmem_limit_bytes=VMEM_LIMIT_BYTES)


def _log_sigmoid(x):
    return jnp.minimum(x, 0.0) - jnp.log1p(jnp.exp(-jnp.abs(x)))


def _sigmoid(x):
    return 1.0 / (1.0 + jnp.exp(-x))


def _dot(a, b):
    return jnp.dot(a, b, preferred_element_type=f32)


def _dot_nt(a, b):
    return lax.dot_general(a, b, (((1,), (1,)), ((), ())), preferred_element_type=f32)


def _dot_tn(a, b):
    return lax.dot_general(a, b, (((0,), (0,)), ((), ())), preferred_element_type=f32)


def _split_bf16(x, n):
    parts = []
    r = x
    for _ in range(n):
        p = r.astype(bf16)
        parts.append(p)
        r = r - p.astype(f32)
    return parts


def _in_proj_kernel(x_ref, g_ref, w_ref, qkv_ref, ff_ref, hqig_ref, hf_ref, cab_ref, gl_ref):
    x = x_ref[...]
    h = (x * lax.rsqrt(jnp.mean(x * x, axis=-1, keepdims=True) + EPS) * g_ref[...]).astype(bf16)

    c0 = 0
    z = _dot(h, w_ref[:, c0:c0 + QKV_W + FF_W])
    qkv_ref[...] = z[:, :QKV_W].astype(bf16)
    ff_ref[...] = z[:, QKV_W:]
    c0 += QKV_W + FF_W
    hqig_ref[...] = _dot(h, w_ref[:, c0:c0 + HQIG_W]).astype(bf16)
    c0 += HQIG_W
    hf_ref[...] = _dot(h, w_ref[:, c0:c0 + HF_W])
    c0 += HF_W
    cab_ref[...] = _dot(h, w_ref[:, c0:c0 + CAB_W]).astype(bf16)
    c0 += CAB_W
    half = GL_W // 2
    for a in (0, half):
        gl_ref[:, a:a + half] = _dot(h, w_ref[:, c0 + a:c0 + a + half]).astype(bf16)


def _in_proj(x2d, g, w, tm):
    m = x2d.shape[0]
    widths = (QKV_W, FF_W, HQIG_W, HF_W, CAB_W, GL_W)
    dtypes = (bf16, f32, bf16, f32, bf16, bf16)
    return pl.pallas_call(
        _in_proj_kernel,
        grid=(m // tm,),
        in_specs=[
            pl.BlockSpec((tm, D_MODEL), lambda i: (i, 0)),
            pl.BlockSpec((1, D_MODEL), lambda i: (0, 0)),
            pl.BlockSpec((D_MODEL, IN_PROJ_W), lambda i: (0, 0), pipeline_mode=pl.Buffered(1)),
        ],
        out_specs=[pl.BlockSpec((tm, wd), lambda i: (i, 0)) for wd in widths],
        out_shape=[jax.ShapeDtypeStruct((m, wd), dt) for wd, dt in zip(widths, dtypes)],
        compiler_params=_cparams(("parallel",)),
        name="in_proj",
    )(x2d, g, w)


def _fox_prep_kernel(qkv_ref, ff_ref, fb_ref, gq_ref, gk_ref, tri_ref,
                     q_out, k_out, vt_out, carry_ref, *, ts):
    @pl.when(pl.program_id(1) == 0)
    def _():
        carry_ref[...] = jnp.zeros_like(carry_ref)

    lane = lax.broadcasted_iota(jnp.int32, (ts, LANE), 1)
    low = lane < FOX_HEAD_DIM

    lf = _log_sigmoid(ff_ref[0] + fb_ref[...])
    parts = jnp.concatenate(_split_bf16(lf, 3), axis=1)
    cs = _dot(tri_ref[...], parts)
    c = (cs[:, :LANE] + carry_ref[...]) + cs[:, LANE:2 * LANE] + cs[:, 2 * LANE:]
    carry_ref[...] = c[ts - 1:ts, :]

    def normed(col0, gain_ref, scale):
        blocks = []
        for j in range(FOX_PAIRS):
            blk = qkv_ref[0, :, col0 + j * LANE:col0 + (j + 1) * LANE].astype(f32)
            sq = blk * blk
            s_lo = jnp.sum(jnp.where(low, sq, 0.0), axis=1, keepdims=True)
            s_hi = jnp.sum(jnp.where(low, 0.0, sq), axis=1, keepdims=True)
            rs = jnp.where(low, lax.rsqrt(s_lo * (1.0 / FOX_HEAD_DIM) + EPS),
                           lax.rsqrt(s_hi * (1.0 / FOX_HEAD_DIM) + EPS))
            blocks.append(blk * rs * (gain_ref[:, j * LANE:(j + 1) * LANE] * scale))
        return blocks

    qn = normed(0, gq_ref, FOX_HEAD_DIM ** -0.5)
    kn = normed(FOX_WIDTH, gk_ref, 1.0)

    for h in range(FOX_HEADS):
        cb = jnp.broadcast_to(c[:, h:h + 1], (ts, LANE))
        c3 = [p.astype(f32) for p in _split_bf16(cb, 3)]

        def head_part(blocks):
            blk = blocks[h // 2]
            if h % 2 == 1:
                blk = pltpu.roll(blk, FOX_HEAD_DIM, axis=1)
            return jnp.where(low, blk, 0.0)

        qa = head_part(qn)
        ka = head_part(kn)
        for i in range(3):
            qa = jnp.where(lane == FOX_HEAD_DIM + i, c3[i], qa)
            qa = jnp.where(lane == FOX_HEAD_DIM + 3 + i, 1.0, qa)
            ka = jnp.where(lane == FOX_HEAD_DIM + i, 1.0, ka)
            ka = jnp.where(lane == FOX_HEAD_DIM + 3 + i, -c3[i], ka)
        q_out[0, h] = qa.astype(bf16)
        k_out[0, h] = ka.astype(bf16)

    vt = qkv_ref[0, :, 2 * FOX_WIDTH:3 * FOX_WIDTH].astype(f32).T
    tail_row = lax.broadcasted_iota(jnp.int32, (FOX_VT_ROWS - FOX_HEAD_DIM, ts), 0)
    tail = jnp.where(tail_row == 0, 1.0, 0.0).astype(bf16)
    for h in range(FOX_HEADS):
        vt_out[0, h, 0:FOX_HEAD_DIM, :] = vt[h * FOX_HEAD_DIM:(h + 1) * FOX_HEAD_DIM, :].astype(bf16)
        vt_out[0, h, FOX_HEAD_DIM:FOX_VT_ROWS, :] = tail


def _fox_prep(qkv, ff, fb, gq, gk, tri, ts):
    b, s, _ = qkv.shape
    kern = functools.partial(_fox_prep_kernel, ts=ts)
    return pl.pallas_call(
        kern,
        grid=(b, s // ts),
        in_specs=[
            pl.BlockSpec((1, ts, QKV_W), lambda i, j: (i, j, 0)),
            pl.BlockSpec((1, ts, FF_W), lambda i, j: (i, j, 0)),
            pl.BlockSpec((1, LANE), lambda i, j: (0, 0)),
            pl.BlockSpec((1, FOX_WIDTH), lambda i, j: (0, 0)),
            pl.BlockSpec((1, FOX_WIDTH), lambda i, j: (0, 0)),
            pl.BlockSpec((ts, ts), lambda i, j: (0, 0)),
        ],
        out_specs=[
            pl.BlockSpec((1, FOX_HEADS, ts, FOX_AUG), lambda i, j: (i, 0, j, 0)),
            pl.BlockSpec((1, FOX_HEADS, ts, FOX_AUG), lambda i, j: (i, 0, j, 0)),
            pl.BlockSpec((1, FOX_HEADS, FOX_VT_ROWS, ts), lambda i, j: (i, 0, 0, j)),
        ],
        out_shape=[
            jax.ShapeDtypeStruct((b, FOX_HEADS, s, FOX_AUG), bf16),
            jax.ShapeDtypeStruct((b, FOX_HEADS, s, FOX_AUG), bf16),
            jax.ShapeDtypeStruct((b, FOX_HEADS, FOX_VT_ROWS, s), bf16),
        ],
        scratch_shapes=[pltpu.VMEM((1, LANE), f32)],
        compiler_params=_cparams(("parallel", "arbitrary")),
        name="fox_prep",
    )(qkv, ff, fb, gq, gk, tri)


def _fox_attn_kernel(q_ref, k_ref, vt_ref, o_ref, *, tq):
    qi = pl.program_id(2)
    key_idx = lax.broadcasted_iota(jnp.int32, (tq, tq), 0)
    qry_idx = lax.broadcasted_iota(jnp.int32, (tq, tq), 1)
    causal = key_idx <= qry_idx

    outs = []
    for hh in range(2):
        q = q_ref[0, hh]

        def step(kb, carry, masked):
            m, acc = carry
            start = pl.multiple_of(kb * tq, tq)
            st = _dot_nt(k_ref[0, hh, pl.ds(start, tq), :], q)
            if masked:
                st = jnp.where(causal, st, NEG_BIG)
            m_new = jnp.maximum(m, jnp.max(st, axis=0, keepdims=True))
            p = jnp.exp(st - m_new)
            alpha = jnp.exp(m - m_new)
            pv = _dot(vt_ref[0, hh, :, pl.ds(start, tq)], p.astype(bf16))
            return m_new, alpha * acc + pv

        init = (jnp.full((1, tq), NEG_BIG, f32), jnp.zeros((FOX_VT_ROWS, tq), f32))
        carry = lax.fori_loop(0, qi, functools.partial(step, masked=False), init)
        _, acc = step(qi, carry, True)
        outs.append(acc[:FOX_HEAD_DIM] / acc[FOX_HEAD_DIM:FOX_HEAD_DIM + 1])
    o_ref[0] = jnp.concatenate(outs, axis=0).T.astype(bf16)


def _fox_attn(qa, ka, vt, tq):
    b, _, s, _ = qa.shape
    kern = functools.partial(_fox_attn_kernel, tq=tq)
    return pl.pallas_call(
        kern,
        grid=(b, FOX_PAIRS, s // tq),
        in_specs=[
            pl.BlockSpec((1, 2, tq, FOX_AUG), lambda i, p, j: (i, p, j, 0)),
            pl.BlockSpec((1, 2, s, FOX_AUG), lambda i, p, j: (i, p, 0, 0)),
            pl.BlockSpec((1, 2, FOX_VT_ROWS, s), lambda i, p, j: (i, p, 0, 0)),
        ],
        out_specs=pl.BlockSpec((1, tq, LANE), lambda i, p, j: (i, j, p)),
        out_shape=jax.ShapeDtypeStruct((b, s, FOX_WIDTH), bf16),
        compiler_params=_cparams(("parallel", "parallel", "arbitrary")),
        name="fox_attn",
    )(qa, ka, vt)


_HGRN_HALVES = (1, 2, 4, 8, 16, 32)
_HGRN_MXU_LEVELS = 3


def _hgrn_prefix_matrix():
    mats = []
    for hsz in _HGRN_HALVES[:_HGRN_MXU_LEVELS]:
        m = np.zeros((CHUNK, CHUNK), np.float32)
        for t in range(CHUNK):
            bd = (t // (2 * hsz)) * 2 * hsz + hsz - 1
            if t % (2 * hsz) >= hsz:
                m[t, bd + 1:t + 1] = 1.0
            else:
                m[t, t + 1:bd + 1] = 1.0
        mats.append(m)
    mats.append(np.tril(np.ones((CHUNK, CHUNK), np.float32)))
    return np.concatenate(mats, axis=0)


def _hgrn_kernel(hqig_ref, hf_ref, llb_ref, l1m_ref, gn_ref, cmat_ref, y_ref, st_ref, *, n_chunks):
    @pl.when(pl.program_id(1) == 0)
    def _():
        st_ref[...] = jnp.zeros_like(st_ref)

    t_i = lax.broadcasted_iota(jnp.int32, (CHUNK, CHUNK), 0)
    s_i = lax.broadcasted_iota(jnp.int32, (CHUNK, CHUNK), 1)
    masks = []
    for hsz in _HGRN_HALVES:
        same = (t_i // (2 * hsz)) == (s_i // (2 * hsz))
        masks.append(same & ((t_i % (2 * hsz)) >= hsz) & ((s_i % (2 * hsz)) < hsz))
    cmat = cmat_ref[...]
    kw, vw = HGRN_KEY_WIDTH, HGRN_PAD_WIDTH

    def chunk(c, carry):
        r0 = pl.multiple_of(c * CHUNK, CHUNK)
        rows = pl.ds(r0, CHUNK)
        for h in range(HGRN_HEADS):
            kcols = slice(h * HGRN_KEY_DIM, (h + 1) * HGRN_KEY_DIM)
            q = hqig_ref[0, rows, kcols].astype(f32)
            v = hqig_ref[0, rows, kw + h * HGRN_VAL_PAD:kw + (h + 1) * HGRN_VAL_PAD]
            g = hqig_ref[0, rows, kw + vw + h * HGRN_VAL_PAD:kw + vw + (h + 1) * HGRN_VAL_PAD].astype(f32)
            x = hf_ref[0, rows, kcols]

            ls = _log_sigmoid(x)
            a = llb_ref[:, kcols]
            bb = l1m_ref[:, kcols] + ls
            logf = jnp.maximum(a, bb) + jnp.log1p(jnp.exp(-jnp.abs(a - bb)))
            kk = jnp.exp(l1m_ref[:, kcols] + ls - x)

            lh = jnp.concatenate(_split_bf16(logf, 2), axis=1)
            e = _dot(cmat, lh)
            e = e[:, :HGRN_KEY_DIM] + e[:, HGRN_KEY_DIM:]
            nl = _HGRN_MXU_LEVELS
            b = e[nl * CHUNK:(nl + 1) * CHUNK]
            decays = [jnp.exp(e[l * CHUNK:(l + 1) * CHUNK]) for l in range(nl)]
            for hsz in _HGRN_HALVES[nl:]:
                bc = jnp.concatenate(
                    [jnp.broadcast_to(b[blk + hsz - 1:blk + hsz, :], (2 * hsz, HGRN_KEY_DIM))
                     for blk in range(0, CHUNK, 2 * hsz)], axis=0)
                decays.append(jnp.exp(-jnp.abs(b - bc)))

            attn = jnp.zeros((CHUNK, CHUNK), f32)
            for mask, dec in zip(masks, decays):
                sc = _dot_nt((q * dec).astype(bf16), (kk * dec).astype(bf16))
                attn = attn + jnp.where(mask, sc, 0.0)
            diag = jnp.sum(q * kk, axis=1, keepdims=True)

            state_t = st_ref[h]
            o = (_dot(attn.astype(bf16), v) + diag * v.astype(f32)
                 + _dot_nt((q * jnp.exp(b)).astype(bf16), state_t.astype(bf16)))
            b_last = b[CHUNK - 1:CHUNK, :]
            kd = (kk * jnp.exp(b_last - b)).astype(bf16)
            st_ref[h] = jnp.exp(b_last) * state_t + _dot_tn(v, kd)

            ms = jnp.sum(o * o, axis=1, keepdims=True) * (1.0 / HGRN_VAL_DIM)
            vcols = slice(h * HGRN_VAL_PAD, (h + 1) * HGRN_VAL_PAD)
            y = o * lax.rsqrt(ms + EPS) * gn_ref[:, vcols] * (g * _sigmoid(g))
            y_ref[0, rows, vcols] = y.astype(bf16)
        return carry

    lax.fori_loop(0, n_chunks, chunk, 0)


def _hgrn(hqig, hf, llb, l1m, gn, cmat, tb):
    b, s, _ = hqig.shape
    kern = functools.partial(_hgrn_kernel, n_chunks=tb // CHUNK)
    return pl.pallas_call(
        kern,
        grid=(b, s // tb),
        in_specs=[
            pl.BlockSpec((1, tb, HQIG_W), lambda i, j: (i, j, 0)),
            pl.BlockSpec((1, tb, HF_W), lambda i, j: (i, j, 0)),
            pl.BlockSpec((1, HGRN_KEY_WIDTH), lambda i, j: (0, 0)),
            pl.BlockSpec((1, HGRN_KEY_WIDTH), lambda i, j: (0, 0)),
            pl.BlockSpec((1, HGRN_PAD_WIDTH), lambda i, j: (0, 0)),
            pl.BlockSpec(cmat.shape, lambda i, j: (0, 0)),
        ],
        out_specs=pl.BlockSpec((1, tb, HGRN_PAD_WIDTH), lambda i, j: (i, j, 0)),
        out_shape=jax.ShapeDtypeStruct((b, s, HGRN_PAD_WIDTH), bf16),
        scratch_shapes=[pltpu.VMEM((HGRN_HEADS, HGRN_VAL_PAD, HGRN_KEY_DIM), f32)],
        compiler_params=_cparams(("parallel", "arbitrary")),
        name="hgrn",
    )(hqig, hf, llb, l1m, gn, cmat)


_CONV_HIST = 32
_CONV_ROWS = 64


def _conv_kernel(cab_ref, w_ref, b_ref, g_ref, beta_ref, y_ref, ubuf_ref, *, ts):
    @pl.when(pl.program_id(1) == 0)
    def _():
        ubuf_ref[0:_CONV_HIST, :] = jnp.zeros((_CONV_HIST, CONV_CH), f32)

    ca = cab_ref[0, :, 0:CONV_CH].astype(f32)
    cb = cab_ref[0, :, CONV_CH:2 * CONV_CH].astype(f32)
    ubuf_ref[_CONV_HIST:_CONV_HIST + ts, :] = ca * _sigmoid(cb)

    first = _CONV_HIST - (CONV_WIDTH - 1)
    for r0 in range(0, ts, _CONV_ROWS):
        acc = jnp.broadcast_to(b_ref[...], (_CONV_ROWS, CONV_CH))
        for j in range(CONV_WIDTH):
            acc = acc + w_ref[j:j + 1, :] * ubuf_ref[r0 + first + j:r0 + first + j + _CONV_ROWS, :]
        mu = jnp.mean(acc, axis=-1, keepdims=True)
        xc = acc - mu
        yn = xc * lax.rsqrt(jnp.mean(xc * xc, axis=-1, keepdims=True) + EPS)
        yn = yn * g_ref[...] + beta_ref[...]
        y_ref[0, r0:r0 + _CONV_ROWS, :] = (yn * _sigmoid(yn)).astype(bf16)

    ubuf_ref[0:_CONV_HIST, :] = ubuf_ref[ts:ts + _CONV_HIST, :]


def _conv(cab, w, bias, g, beta, ts):
    b, s, _ = cab.shape
    kern = functools.partial(_conv_kernel, ts=ts)
    vec = pl.BlockSpec((1, CONV_CH), lambda i, j: (0, 0))
    return pl.pallas_call(
        kern,
        grid=(b, s // ts),
        in_specs=[
            pl.BlockSpec((1, ts, CAB_W), lambda i, j: (i, j, 0)),
            pl.BlockSpec((CONV_WIDTH, CONV_CH), lambda i, j: (0, 0)),
            vec, vec, vec,
        ],
        out_specs=pl.BlockSpec((1, ts, CONV_CH), lambda i, j: (i, j, 0)),
        out_shape=jax.ShapeDtypeStruct((b, s, CONV_CH), bf16),
        scratch_shapes=[pltpu.VMEM((ts + _CONV_HIST, CONV_CH), f32)],
        compiler_params=_cparams(("parallel", "arbitrary")),
        name="conv",
    )(cab, w, bias, g, beta)


def _merge_kernel(ya_ref, yb_ref, yc_ref, gl_ref, gb_ref, x_ref, wa_ref, wb_ref, wc_ref, wo_ref,
                  gn_ref, xo_ref, h_ref):
    merged = None
    for i, (y_ref, w_ref) in enumerate(((ya_ref, wa_ref), (yb_ref, wb_ref), (yc_ref, wc_ref))):
        cols = slice(i * D_MODEL, (i + 1) * D_MODEL)
        gate = _sigmoid(gl_ref[:, cols].astype(f32) + gb_ref[:, cols])
        term = gate * _dot(y_ref[...], w_ref[...])
        merged = term if merged is None else merged + term
    x = x_ref[...] + _dot(merged.astype(bf16), wo_ref[...])
    xo_ref[...] = x
    h_ref[...] = (x * lax.rsqrt(jnp.mean(x * x, axis=-1, keepdims=True) + EPS) * gn_ref[...]).astype(bf16)


def _merge(ya, yb, yc, gl, gate_b, x2d, wa, wb, wc, wo, gn, tm):
    m = x2d.shape[0]
    row = lambda wd: pl.BlockSpec((tm, wd), lambda i: (i, 0))
    full = lambda a: pl.BlockSpec(a.shape, lambda i: (0, 0))
    return pl.pallas_call(
        _merge_kernel,
        grid=(m // tm,),
        in_specs=[row(FOX_WIDTH), row(HGRN_PAD_WIDTH), row(CONV_CH), row(GL_W), full(gate_b),
                  row(D_MODEL), full(wa), full(wb), full(wc), full(wo), full(gn)],
        out_specs=[row(D_MODEL), row(D_MODEL)],
        out_shape=[jax.ShapeDtypeStruct((m, D_MODEL), f32), jax.ShapeDtypeStruct((m, D_MODEL), bf16)],
        compiler_params=_cparams(("parallel",)),
        name="merge",
    )(ya, yb, yc, gl, gate_b, x2d, wa, wb, wc, wo, gn)


_FFN_HALO = SUBLANE


def _ffn_kernel(h_ref, hprev_ref, x_ref, wg_ref, wv_ref, cw_ref, cb_ref, wd_ref, o_ref, gbuf_ref,
                *, tm, blocks_per_seq):
    i = pl.program_id(0)
    j = pl.program_id(1)
    h = h_ref[...]
    g = _dot(h, wg_ref[...])
    gprev = _dot(hprev_ref[...], wg_ref[...])
    seq_start = (i % blocks_per_seq) == 0
    gbuf_ref[0:_FFN_HALO, :] = jnp.where(seq_start, 0.0, gprev)
    gbuf_ref[_FFN_HALO:_FFN_HALO + tm, :] = g
    conv = cb_ref[...] + cw_ref[FFN_CONV_WIDTH - 1:FFN_CONV_WIDTH, :] * g
    for k in range(FFN_CONV_WIDTH - 1):
        off = _FFN_HALO - (FFN_CONV_WIDTH - 1) + k
        conv = conv + cw_ref[k:k + 1, :] * gbuf_ref[off:off + tm, :]
    act = (conv * _sigmoid(conv) * _dot(h, wv_ref[...])).astype(bf16)
    contrib = _dot(act, wd_ref[...])

    @pl.when(j == 0)
    def _():
        o_ref[...] = x_ref[...] + contrib

    @pl.when(j > 0)
    def _():
        o_ref[...] += contrib


def _ffn(h2, x2d, wg, wv, cw, cb, wd, tm, tf, seq):
    m = x2d.shape[0]
    kern = functools.partial(_ffn_kernel, tm=tm, blocks_per_seq=seq // tm)
    halo_blocks = tm // _FFN_HALO
    return pl.pallas_call(
        kern,
        grid=(m // tm, D_FF // tf),
        in_specs=[
            pl.BlockSpec((tm, D_MODEL), lambda i, j: (i, 0)),
            pl.BlockSpec((_FFN_HALO, D_MODEL), lambda i, j: (jnp.maximum(i * halo_blocks - 1, 0), 0)),
            pl.BlockSpec((tm, D_MODEL), lambda i, j: (i, 0)),
            pl.BlockSpec((D_MODEL, tf), lambda i, j: (0, j)),
            pl.BlockSpec((D_MODEL, tf), lambda i, j: (0, j)),
            pl.BlockSpec((FFN_CONV_WIDTH, tf), lambda i, j: (0, j)),
            pl.BlockSpec((1, tf), lambda i, j: (0, j)),
            pl.BlockSpec((tf, D_MODEL), lambda i, j: (j, 0)),
        ],
        out_specs=pl.BlockSpec((tm, D_MODEL), lambda i, j: (i, 0)),
        out_shape=jax.ShapeDtypeStruct((m, D_MODEL), f32),
        scratch_shapes=[pltpu.VMEM((tm + _FFN_HALO, tf), f32)],
        compiler_params=_cparams(("parallel", "arbitrary")),
        name="ffn",
    )(h2, h2, x2d, wg, wv, cw, cb, wd)


def _pad_heads(w, axis):
    shape = list(w.shape)
    shape[axis:axis + 1] = [HGRN_HEADS, HGRN_VAL_DIM]
    w = w.reshape(shape)
    pad = [(0, 0)] * len(shape)
    pad[axis + 1] = (0, HGRN_VAL_PAD - HGRN_VAL_DIM)
    w = jnp.pad(w, pad)
    shape[axis:axis + 2] = [HGRN_PAD_WIDTH]
    return w.reshape(shape)


def _block(total, want):
    return min(total, want)


@jax.jit
def _forward(x, norm_mix_g, w_in, fox_forget_b, fox_q_norm_g, fox_k_norm_g, hgrn_lb_logits,
             hgrn_out_norm_g, conv_dw_w, conv_dw_b, conv_norm_g, conv_norm_b, gate_b, w_branch,
             w_out, norm_ffn_g, w_up, ffn_dw_w, ffn_dw_b, w_down):
    b, s, d = x.shape
    depth = w_in.shape[0]
    m = b * s

    tm_in = _block(m, 256)
    ts_prep = _block(s, 512)
    tq = _block(s, 256)
    tb_hgrn = _block(s, 512)
    ts_conv = _block(s, 256)
    tm_merge = _block(m, 512)
    tm_ffn = _block(s, 1024)
    tf_ffn = 256

    lb_all = jnp.cumsum(jax.nn.softmax(hgrn_lb_logits.astype(f32), axis=0), axis=0)
    lb_all = jnp.maximum(lb_all - lb_all[0:1], 0.0)
    log_lb = jnp.log(lb_all)
    log_1m_lb = jnp.log1p(-lb_all)

    tri = jnp.asarray(np.tril(np.ones((ts_prep, ts_prep), np.float32)), bf16)
    cmat = jnp.asarray(_hgrn_prefix_matrix(), bf16)
    o_a = FOX_WIDTH
    o_b = FOX_WIDTH + HGRN_VAL_WIDTH

    x2d = x.reshape(m, d)
    for l in range(depth):
        fq, fk, fv, ffw, hq, hfw, hi, hg, ca, cbw, glw = jnp.split(w_in[l], _IN_SPLITS, axis=1)
        ffw = jnp.pad(ffw, ((0, 0), (0, FF_W - FOX_HEADS)))
        w_cat = jnp.concatenate(
            [fq, fk, fv, ffw, hq, _pad_heads(hi, 1), _pad_heads(hg, 1), hfw, ca, cbw, glw], axis=1).astype(bf16)

        qkv, ff, hqig, hf, cab, gl = _in_proj(x2d, norm_mix_g[l][None, :], w_cat, tm_in)

        fb = jnp.pad(fox_forget_b[l], (0, LANE - FOX_HEADS))[None, :]
        gq = jnp.tile(fox_q_norm_g[l], FOX_HEADS)[None, :]
        gk = jnp.tile(fox_k_norm_g[l], FOX_HEADS)[None, :]
        qa, ka, vt = _fox_prep(qkv.reshape(b, s, QKV_W), ff.reshape(b, s, FF_W), fb, gq, gk, tri, ts_prep)
        ya = _fox_attn(qa, ka, vt, tq)

        gn = jnp.tile(jnp.pad(hgrn_out_norm_g[l], (0, HGRN_VAL_PAD - HGRN_VAL_DIM)), HGRN_HEADS)[None, :]
        yb = _hgrn(hqig.reshape(b, s, HQIG_W), hf.reshape(b, s, HF_W), log_lb[l][None, :],
                   log_1m_lb[l][None, :], gn, cmat, tb_hgrn)

        yc = _conv(cab.reshape(b, s, CAB_W), conv_dw_w[l], conv_dw_b[l][None, :],
                   conv_norm_g[l][None, :], conv_norm_b[l][None, :], ts_conv)

        wb = w_branch[l]
        x2d, h2 = _merge(ya.reshape(m, FOX_WIDTH), yb.reshape(m, HGRN_PAD_WIDTH), yc.reshape(m, CONV_CH),
                         gl, gate_b[l][None, :], x2d,
                         wb[:o_a].astype(bf16), _pad_heads(wb[o_a:o_b], 0).astype(bf16),
                         wb[o_b:].astype(bf16), w_out[l].astype(bf16), norm_ffn_g[l][None, :], tm_merge)

        wup = w_up[l].astype(bf16)
        x2d = _ffn(h2, x2d, wup[:, :D_FF], wup[:, D_FF:], ffn_dw_w[l], ffn_dw_b[l][None, :],
                   w_down[l].astype(bf16), tm_ffn, tf_ffn, s)
    return x2d.reshape(b, s, d)


_IN_SIZES = (FOX_WIDTH, FOX_WIDTH, FOX_WIDTH, FOX_HEADS, HGRN_KEY_WIDTH, HGRN_KEY_WIDTH,
             HGRN_VAL_WIDTH, HGRN_VAL_WIDTH, CONV_CH, CONV_CH, N_BRANCH * D_MODEL)
_IN_SPLITS = tuple(int(v) for v in np.cumsum(_IN_SIZES)[:-1])


def kernel(x, norm_mix_g, w_in, fox_forget_b, fox_q_norm_g, fox_k_norm_g, hgrn_lb_logits,
           hgrn_out_norm_g, conv_dw_w, conv_dw_b, conv_norm_g, conv_norm_b, gate_b, w_branch, w_out,
           norm_ffn_g, w_up, ffn_dw_w, ffn_dw_b, w_down):
    return _forward(x, norm_mix_g, w_in, fox_forget_b, fox_q_norm_g, fox_k_norm_g, hgrn_lb_logits,
                    hgrn_out_norm_g, conv_dw_w, conv_dw_b, conv_norm_g, conv_norm_b, gate_b, w_branch,
                    w_out, norm_ffn_g, w_up, ffn_dw_w, ffn_dw_b, w_down)
```

```python
import functools

import numpy as np
import jax
import jax.numpy as jnp
from jax import lax
from jax.experimental import pallas as pl
from jax.experimental.pallas import tpu as pltpu

D_MODEL = 1024
CHUNK = 64
FOX_HEADS = 6
FOX_HEAD_DIM = 64
FOX_WIDTH = FOX_HEADS * FOX_HEAD_DIM
HGRN_HEADS = 4
HGRN_KEY_DIM = 128
HGRN_VAL_DIM = 96
HGRN_KEY_WIDTH = HGRN_HEADS * HGRN_KEY_DIM
HGRN_VAL_WIDTH = HGRN_HEADS * HGRN_VAL_DIM
CONV_CH = 256
CONV_WIDTH = 31
N_BRANCH = 3
D_FF = 2816
FFN_CONV_WIDTH = 3
EPS = 1e-6

LANE = 128
SUBLANE = 8
VMEM_LIMIT_BYTES = 56 * 1024 * 1024

HGRN_VAL_PAD = LANE
HGRN_PAD_WIDTH = HGRN_HEADS * HGRN_VAL_PAD
FOX_AUG = LANE
FOX_VT_ROWS = 80
FOX_PAIRS = FOX_HEADS // 2
NEG_BIG = -1e30
LOG2_E = 1.4426950408889634

QKV_W = 3 * FOX_WIDTH
FF_W = LANE
HQIG_W = HGRN_KEY_WIDTH + 2 * HGRN_PAD_WIDTH
HF_W = HGRN_KEY_WIDTH
CAB_W = 2 * CONV_CH
GL_W = N_BRANCH * D_MODEL
IN_PROJ_W = QKV_W + FF_W + HQIG_W + HF_W + CAB_W + GL_W

f32 = jnp.float32
bf16 = jnp.bfloat16


def _cparams(semantics):
    return pltpu.CompilerParams(dimension_semantics=semantics, vmem_limit_bytes=VMEM_LIMIT_BYTES)


def _log_sigmoid(x):
    return jnp.minimum(x, 0.0) - jnp.log1p(jnp.exp(-jnp.abs(x)))


def _sigmoid(x):
    return 1.0 / (1.0 + jnp.exp(-x))


def _dot(a, b):
    return jnp.dot(a, b, preferred_element_type=f32)


def _dot_nt(a, b):
    return lax.dot_general(a, b, (((1,), (1,)), ((), ())), preferred_element_type=f32)


def _dot_tn(a, b):
    return lax.dot_general(a, b, (((0,), (0,)), ((), ())), preferred_element_type=f32)


def _split_bf16(x, n):
    parts = []
    r = x
    for _ in range(n):
        p = r.astype(bf16)
        parts.append(p)
        r = r - p.astype(f32)
    return parts


_IN_SIZES = (FOX_WIDTH, FOX_WIDTH, FOX_WIDTH, FOX_HEADS, HGRN_KEY_WIDTH, HGRN_KEY_WIDTH,
             HGRN_VAL_WIDTH, HGRN_VAL_WIDTH, CONV_CH, CONV_CH, N_BRANCH * D_MODEL)
_IN_OFFS = tuple(int(v) for v in np.cumsum((0,) + _IN_SIZES))
_W_PREP_ROWS = 128


def _w_in_prep_kernel(w_ref, o_ref):
    o_fq, _, _, o_ff, o_hq, o_hf, o_hi, o_hg, o_ca, _, o_gl, _ = _IN_OFFS
    rows = w_ref.shape[1]

    def src(a, width):
        return w_ref[0, :, a:a + width]

    def padded_heads(a):
        pieces = []
        for h in range(HGRN_HEADS):
            pieces.append(src(a + h * HGRN_VAL_DIM, HGRN_VAL_DIM))
            pieces.append(jnp.zeros((rows, HGRN_VAL_PAD - HGRN_VAL_DIM), f32))
        return jnp.concatenate(pieces, axis=1)

    segments = [
        src(o_fq, QKV_W),
        jnp.concatenate([src(o_ff, FOX_HEADS), jnp.zeros((rows, FF_W - FOX_HEADS), f32)], axis=1),
        src(o_hq, HGRN_KEY_WIDTH),
        padded_heads(o_hi),
        padded_heads(o_hg),
        src(o_hf, HGRN_KEY_WIDTH),
        src(o_ca, CAB_W),
        src(o_gl, GL_W),
    ]
    c0 = 0
    for seg in segments:
        o_ref[0, :, c0:c0 + seg.shape[1]] = seg.astype(bf16)
        c0 += seg.shape[1]


def _w_in_prep(w_in):
    depth, d, n = w_in.shape
    return pl.pallas_call(
        _w_in_prep_kernel,
        grid=(depth, d // _W_PREP_ROWS),
        in_specs=[pl.BlockSpec((1, _W_PREP_ROWS, n), lambda l, i: (l, i, 0))],
        out_specs=pl.BlockSpec((1, _W_PREP_ROWS, IN_PROJ_W), lambda l, i: (l, i, 0)),
        out_shape=jax.ShapeDtypeStruct((depth, d, IN_PROJ_W), bf16),
        compiler_params=_cparams(("parallel", "parallel")),
        name="w_in_prep",
    )(w_in)


def _in_proj_kernel(x_ref, g_ref, w_ref, qkv_ref, ff_ref, hqig_ref, hf_ref, cab_ref, gl_ref):
    x = x_ref[...]
    h = (x * lax.rsqrt(jnp.mean(x * x, axis=-1, keepdims=True) + EPS) * g_ref[...]).astype(bf16)

    c0 = 0
    z = _dot(h, w_ref[:, c0:c0 + QKV_W + FF_W])
    qkv_ref[...] = z[:, :QKV_W].astype(bf16)
    ff_ref[...] = z[:, QKV_W:]
    c0 += QKV_W + FF_W
    hqig_ref[...] = _dot(h, w_ref[:, c0:c0 + HQIG_W]).astype(bf16)
    c0 += HQIG_W
    hf_ref[...] = _dot(h, w_ref[:, c0:c0 + HF_W])
    c0 += HF_W
    cab_ref[...] = _dot(h, w_ref[:, c0:c0 + CAB_W]).astype(bf16)
    c0 += CAB_W
    half = GL_W // 2
    for a in (0, half):
        gl_ref[:, a:a + half] = _dot(h, w_ref[:, c0 + a:c0 + a + half]).astype(bf16)


def _in_proj(x2d, g, w_all, layer, tm):
    m = x2d.shape[0]
    widths = (QKV_W, FF_W, HQIG_W, HF_W, CAB_W, GL_W)
    dtypes = (bf16, f32, bf16, f32, bf16, bf16)
    return pl.pallas_call(
        _in_proj_kernel,
        grid=(m // tm,),
        in_specs=[
            pl.BlockSpec((tm, D_MODEL), lambda i: (i, 0)),
            pl.BlockSpec((1, D_MODEL), lambda i: (0, 0)),
            pl.BlockSpec((None, D_MODEL, IN_PROJ_W), lambda i: (layer, 0, 0), pipeline_mode=pl.Buffered(1)),
        ],
        out_specs=[pl.BlockSpec((tm, wd), lambda i: (i, 0)) for wd in widths],
        out_shape=[jax.ShapeDtypeStruct((m, wd), dt) for wd, dt in zip(widths, dtypes)],
        compiler_params=_cparams(("parallel",)),
        name="in_proj",
    )(x2d, g, w_all)


def _fox_prep_kernel(qkv_ref, ff_ref, fb_ref, gq_ref, gk_ref, tri_ref,
                     q_out, k_out, vt_out, carry_ref, *, ts):
    @pl.when(pl.program_id(1) == 0)
    def _():
        carry_ref[...] = jnp.zeros_like(carry_ref)

    lane = lax.broadcasted_iota(jnp.int32, (ts, LANE), 1)
    low = lane < FOX_HEAD_DIM

    lf = _log_sigmoid(ff_ref[0] + fb_ref[...])
    parts = jnp.concatenate(_split_bf16(lf, 3), axis=1)
    cs = _dot(tri_ref[...], parts)
    c = (cs[:, :LANE] + carry_ref[...]) + cs[:, LANE:2 * LANE] + cs[:, 2 * LANE:]
    carry_ref[...] = c[ts - 1:ts, :]

    def normed(col0, gain_ref, scale):
        blocks = []
        for j in range(FOX_PAIRS):
            blk = qkv_ref[0, :, col0 + j * LANE:col0 + (j + 1) * LANE].astype(f32)
            sq = blk * blk
            s_lo = jnp.sum(jnp.where(low, sq, 0.0), axis=1, keepdims=True)
            s_hi = jnp.sum(jnp.where(low, 0.0, sq), axis=1, keepdims=True)
            rs = jnp.where(low, lax.rsqrt(s_lo * (1.0 / FOX_HEAD_DIM) + EPS),
                           lax.rsqrt(s_hi * (1.0 / FOX_HEAD_DIM) + EPS))
            blocks.append(blk * rs * (gain_ref[:, j * LANE:(j + 1) * LANE] * scale))
        return blocks

    qn = normed(0, gq_ref, FOX_HEAD_DIM ** -0.5 * LOG2_E)
    kn = normed(FOX_WIDTH, gk_ref, 1.0)
    c2 = c * LOG2_E

    for h in range(FOX_HEADS):
        cb = jnp.broadcast_to(c2[:, h:h + 1], (ts, LANE))
        c3 = [p.astype(f32) for p in _split_bf16(cb, 3)]

        def head_part(blocks):
            blk = blocks[h // 2]
            if h % 2 == 1:
                blk = pltpu.roll(blk, FOX_HEAD_DIM, axis=1)
            return jnp.where(low, blk, 0.0)

        qa = head_part(qn)
        ka = head_part(kn)
        for i in range(3):
            qa = jnp.where(lane == FOX_HEAD_DIM + i, c3[i], qa)
            qa = jnp.where(lane == FOX_HEAD_DIM + 3 + i, 1.0, qa)
            ka = jnp.where(lane == FOX_HEAD_DIM + i, 1.0, ka)
            ka = jnp.where(lane == FOX_HEAD_DIM + 3 + i, -c3[i], ka)
        q_out[0, h] = qa.astype(bf16)
        k_out[0, h] = ka.astype(bf16)

    vt = qkv_ref[0, :, 2 * FOX_WIDTH:3 * FOX_WIDTH].astype(f32).T
    tail_row = lax.broadcasted_iota(jnp.int32, (FOX_VT_ROWS - FOX_HEAD_DIM, ts), 0)
    tail = jnp.where(tail_row == 0, 1.0, 0.0).astype(bf16)
    for h in range(FOX_HEADS):
        vt_out[0, h, 0:FOX_HEAD_DIM, :] = vt[h * FOX_HEAD_DIM:(h + 1) * FOX_HEAD_DIM, :].astype(bf16)
        vt_out[0, h, FOX_HEAD_DIM:FOX_VT_ROWS, :] = tail


def _fox_prep(qkv, ff, fb, gq, gk, tri, ts):
    b, s, _ = qkv.shape
    kern = functools.partial(_fox_prep_kernel, ts=ts)
    return pl.pallas_call(
        kern,
        grid=(b, s // ts),
        in_specs=[
            pl.BlockSpec((1, ts, QKV_W), lambda i, j: (i, j, 0)),
            pl.BlockSpec((1, ts, FF_W), lambda i, j: (i, j, 0)),
            pl.BlockSpec((1, LANE), lambda i, j: (0, 0)),
            pl.BlockSpec((1, FOX_WIDTH), lambda i, j: (0, 0)),
            pl.BlockSpec((1, FOX_WIDTH), lambda i, j: (0, 0)),
            pl.BlockSpec((ts, ts), lambda i, j: (0, 0)),
        ],
        out_specs=[
            pl.BlockSpec((1, FOX_HEADS, ts, FOX_AUG), lambda i, j: (i, 0, j, 0)),
            pl.BlockSpec((1, FOX_HEADS, ts, FOX_AUG), lambda i, j: (i, 0, j, 0)),
            pl.BlockSpec((1, FOX_HEADS, FOX_VT_ROWS, ts), lambda i, j: (i, 0, 0, j)),
        ],
        out_shape=[
            jax.ShapeDtypeStruct((b, FOX_HEADS, s, FOX_AUG), bf16),
            jax.ShapeDtypeStruct((b, FOX_HEADS, s, FOX_AUG), bf16),
            jax.ShapeDtypeStruct((b, FOX_HEADS, FOX_VT_ROWS, s), bf16),
        ],
        scratch_shapes=[pltpu.VMEM((1, LANE), f32)],
        compiler_params=_cparams(("parallel", "arbitrary")),
        name="fox_prep",
    )(qkv, ff, fb, gq, gk, tri)


def _fox_attn_kernel(q_ref, k_ref, vt_ref, o_ref, m_ref, acc_ref, *, tq, tk):
    qi = pl.program_id(2)
    nsub = tq // tk
    key_idx = lax.broadcasted_iota(jnp.int32, (tk, tk), 0)
    qry_idx = lax.broadcasted_iota(jnp.int32, (tk, tk), 1)
    causal = key_idx <= qry_idx
    m_ref[...] = jnp.full(m_ref.shape, NEG_BIG, f32)
    acc_ref[...] = jnp.zeros(acc_ref.shape, f32)

    def block(hh, kb, lo, diagonal):
        start = pl.multiple_of(kb * tk, tk)
        st = _dot_nt(k_ref[0, hh, pl.ds(start, tk), :], q_ref[0, hh, lo:tq, :])
        if diagonal:
            head = jnp.where(causal, st[:, :tk], NEG_BIG)
            st = head if lo + tk == tq else jnp.concatenate([head, st[:, tk:]], axis=1)
        m_old = m_ref[hh, :, lo:tq]
        m_new = jnp.maximum(m_old, jnp.max(st, axis=0, keepdims=True))
        p = jnp.exp2(st - m_new).astype(bf16)
        pv = _dot(vt_ref[0, hh, :, pl.ds(start, tk)], p)
        acc_ref[hh, :, lo:tq] = jnp.exp2(m_old - m_new) * acc_ref[hh, :, lo:tq] + pv
        m_ref[hh, :, lo:tq] = m_new

    def body(kb, carry):
        for hh in range(2):
            block(hh, kb, 0, False)
        return carry

    lax.fori_loop(0, qi * nsub, body, 0)
    for e in range(nsub):
        for hh in range(2):
            block(hh, qi * nsub + e, e * tk, True)

    outs = [acc_ref[hh, 0:FOX_HEAD_DIM, :] / acc_ref[hh, FOX_HEAD_DIM:FOX_HEAD_DIM + 1, :] for hh in range(2)]
    o_ref[0] = jnp.concatenate(outs, axis=0).T.astype(bf16)


def _fox_attn(qa, ka, vt, tq, tk):
    b, _, s, _ = qa.shape
    kern = functools.partial(_fox_attn_kernel, tq=tq, tk=tk)
    return pl.pallas_call(
        kern,
        grid=(b, FOX_PAIRS, s // tq),
        in_specs=[
            pl.BlockSpec((1, 2, tq, FOX_AUG), lambda i, p, j: (i, p, j, 0)),
            pl.BlockSpec((1, 2, s, FOX_AUG), lambda i, p, j: (i, p, 0, 0)),
            pl.BlockSpec((1, 2, FOX_VT_ROWS, s), lambda i, p, j: (i, p, 0, 0)),
        ],
        out_specs=pl.BlockSpec((1, tq, LANE), lambda i, p, j: (i, j, p)),
        out_shape=jax.ShapeDtypeStruct((b, s, FOX_WIDTH), bf16),
        scratch_shapes=[pltpu.VMEM((2, 1, tq), f32), pltpu.VMEM((2, FOX_VT_ROWS, tq), f32)],
        compiler_params=_cparams(("parallel", "parallel", "arbitrary")),
        name="fox_attn",
    )(qa, ka, vt)


_HGRN_HALVES = (1, 2, 4, 8, 16, 32)
_HGRN_MXU_LEVELS = 3


def _hgrn_prefix_matrix():
    mats = []
    for hsz in _HGRN_HALVES[:_HGRN_MXU_LEVELS]:
        m = np.zeros((CHUNK, CHUNK), np.float32)
        for t in range(CHUNK):
            bd = (t // (2 * hsz)) * 2 * hsz + hsz - 1
            if t % (2 * hsz) >= hsz:
                m[t, bd + 1:t + 1] = 1.0
            else:
                m[t, t + 1:bd + 1] = 1.0
        mats.append(m)
    mats.append(np.tril(np.ones((CHUNK, CHUNK), np.float32)))
    return np.concatenate(mats, axis=0)


def _hgrn_kernel(hqig_ref, hf_ref, llb_ref, l1m_ref, gn_ref, cmat_ref, y_ref, st_ref, *, n_chunks):
    @pl.when(pl.program_id(1) == 0)
    def _():
        st_ref[...] = jnp.zeros_like(st_ref)

    t_i = lax.broadcasted_iota(jnp.int32, (CHUNK, CHUNK), 0)
    s_i = lax.broadcasted_iota(jnp.int32, (CHUNK, CHUNK), 1)
    masks = []
    for hsz in _HGRN_HALVES:
        same = (t_i // (2 * hsz)) == (s_i // (2 * hsz))
        masks.append(same & ((t_i % (2 * hsz)) >= hsz) & ((s_i % (2 * hsz)) < hsz))
    cmat = cmat_ref[...]
    kw, vw = HGRN_KEY_WIDTH, HGRN_PAD_WIDTH

    def chunk(c, carry):
        r0 = pl.multiple_of(c * CHUNK, CHUNK)
        rows = pl.ds(r0, CHUNK)
        for h in range(HGRN_HEADS):
            kcols = slice(h * HGRN_KEY_DIM, (h + 1) * HGRN_KEY_DIM)
            q = hqig_ref[0, rows, kcols].astype(f32)
            v = hqig_ref[0, rows, kw + h * HGRN_VAL_PAD:kw + (h + 1) * HGRN_VAL_PAD]
            g = hqig_ref[0, rows, kw + vw + h * HGRN_VAL_PAD:kw + vw + (h + 1) * HGRN_VAL_PAD].astype(f32)
            x = hf_ref[0, rows, kcols]

            ls = _log_sigmoid(x)
            a = llb_ref[:, kcols]
            bb = l1m_ref[:, kcols] + ls
            logf = jnp.maximum(a, bb) + jnp.log1p(jnp.exp(-jnp.abs(a - bb)))
            kk = jnp.exp(l1m_ref[:, kcols] + ls - x)

            lh = jnp.concatenate(_split_bf16(logf, 2), axis=1)
            e = _dot(cmat, lh)
            e = e[:, :HGRN_KEY_DIM] + e[:, HGRN_KEY_DIM:]
            nl = _HGRN_MXU_LEVELS
            b = e[nl * CHUNK:(nl + 1) * CHUNK]
            decays = [jnp.exp(e[l * CHUNK:(l + 1) * CHUNK]) for l in range(nl)]
            for hsz in _HGRN_HALVES[nl:]:
                bc = jnp.concatenate(
                    [jnp.broadcast_to(b[blk + hsz - 1:blk + hsz, :], (2 * hsz, HGRN_KEY_DIM))
                     for blk in range(0, CHUNK, 2 * hsz)], axis=0)
                decays.append(jnp.exp(-jnp.abs(b - bc)))

            attn = jnp.zeros((CHUNK, CHUNK), f32)
            for mask, dec in zip(masks, decays):
                sc = _dot_nt((q * dec).astype(bf16), (kk * dec).astype(bf16))
                attn = attn + jnp.where(mask, sc, 0.0)
            diag = jnp.sum(q * kk, axis=1, keepdims=True)

            state_t = st_ref[h]
            o = (_dot(attn.astype(bf16), v) + diag * v.astype(f32)
                 + _dot_nt((q * jnp.exp(b)).astype(bf16), state_t.astype(bf16)))
            b_last = b[CHUNK - 1:CHUNK, :]
            kd = (kk * jnp.exp(b_last - b)).astype(bf16)
            st_ref[h] = jnp.exp(b_last) * state_t + _dot_tn(v, kd)

            ms = jnp.sum(o * o, axis=1, keepdims=True) * (1.0 / HGRN_VAL_DIM)
            vcols = slice(h * HGRN_VAL_PAD, (h + 1) * HGRN_VAL_PAD)
            y = o * lax.rsqrt(ms + EPS) * gn_ref[:, vcols] * (g * _sigmoid(g))
            y_ref[0, rows, vcols] = y.astype(bf16)
        return carry

    lax.fori_loop(0, n_chunks, chunk, 0)


def _hgrn(hqig, hf, llb, l1m, gn, cmat, tb):
    b, s, _ = hqig.shape
    kern = functools.partial(_hgrn_kernel, n_chunks=tb // CHUNK)
    return pl.pallas_call(
        kern,
        grid=(b, s // tb),
        in_specs=[
            pl.BlockSpec((1, tb, HQIG_W), lambda i, j: (i, j, 0)),
            pl.BlockSpec((1, tb, HF_W), lambda i, j: (i, j, 0)),
            pl.BlockSpec((1, HGRN_KEY_WIDTH), lambda i, j: (0, 0)),
            pl.BlockSpec((1, HGRN_KEY_WIDTH), lambda i, j: (0, 0)),
            pl.BlockSpec((1, HGRN_PAD_WIDTH), lambda i, j: (0, 0)),
            pl.BlockSpec(cmat.shape, lambda i, j: (0, 0)),
        ],
        out_specs=pl.BlockSpec((1, tb, HGRN_PAD_WIDTH), lambda i, j: (i, j, 0)),
        out_shape=jax.ShapeDtypeStruct((b, s, HGRN_PAD_WIDTH), bf16),
        scratch_shapes=[pltpu.VMEM((HGRN_HEADS, HGRN_VAL_PAD, HGRN_KEY_DIM), f32)],
        compiler_params=_cparams(("parallel", "arbitrary")),
        name="hgrn",
    )(hqig, hf, llb, l1m, gn, cmat)


_CONV_HIST = 32
_CONV_ROWS = 64


def _conv_kernel(cab_ref, w_ref, b_ref, g_ref, beta_ref, y_ref, ubuf_ref, *, ts):
    @pl.when(pl.program_id(1) == 0)
    def _():
        ubuf_ref[0:_CONV_HIST, :] = jnp.zeros((_CONV_HIST, CONV_CH), f32)

    ca = cab_ref[0, :, 0:CONV_CH].astype(f32)
    cb = cab_ref[0, :, CONV_CH:2 * CONV_CH].astype(f32)
    ubuf_ref[_CONV_HIST:_CONV_HIST + ts, :] = ca * _sigmoid(cb)

    first = _CONV_HIST - (CONV_WIDTH - 1)
    for r0 in range(0, ts, _CONV_ROWS):
        acc = jnp.broadcast_to(b_ref[...], (_CONV_ROWS, CONV_CH))
        for j in range(CONV_WIDTH):
            acc = acc + w_ref[j:j + 1, :] * ubuf_ref[r0 + first + j:r0 + first + j + _CONV_ROWS, :]
        mu = jnp.mean(acc, axis=-1, keepdims=True)
        xc = acc - mu
        yn = xc * lax.rsqrt(jnp.mean(xc * xc, axis=-1, keepdims=True) + EPS)
        yn = yn * g_ref[...] + beta_ref[...]
        y_ref[0, r0:r0 + _CONV_ROWS, :] = (yn * _sigmoid(yn)).astype(bf16)

    ubuf_ref[0:_CONV_HIST, :] = ubuf_ref[ts:ts + _CONV_HIST, :]


def _conv(cab, w, bias, g, beta, ts):
    b, s, _ = cab.shape
    kern = functools.partial(_conv_kernel, ts=ts)
    vec = pl.BlockSpec((1, CONV_CH), lambda i, j: (0, 0))
    return pl.pallas_call(
        kern,
        grid=(b, s // ts),
        in_specs=[
            pl.BlockSpec((1, ts, CAB_W), lambda i, j: (i, j, 0)),
            pl.BlockSpec((CONV_WIDTH, CONV_CH), lambda i, j: (0, 0)),
            vec, vec, vec,
        ],
        out_specs=pl.BlockSpec((1, ts, CONV_CH), lambda i, j: (i, j, 0)),
        out_shape=jax.ShapeDtypeStruct((b, s, CONV_CH), bf16),
        scratch_shapes=[pltpu.VMEM((ts + _CONV_HIST, CONV_CH), f32)],
        compiler_params=_cparams(("parallel", "arbitrary")),
        name="conv",
    )(cab, w, bias, g, beta)


def _merge_kernel(ya_ref, yb_ref, yc_ref, gl_ref, gb_ref, x_ref, wa_ref, wb_ref, wc_ref, wo_ref,
                  gn_ref, xo_ref, h_ref):
    merged = None
    for i, (y_ref, w_ref) in enumerate(((ya_ref, wa_ref), (yb_ref, wb_ref), (yc_ref, wc_ref))):
        cols = slice(i * D_MODEL, (i + 1) * D_MODEL)
        gate = _sigmoid(gl_ref[:, cols].astype(f32) + gb_ref[:, cols])
        term = gate * _dot(y_ref[...], w_ref[...])
        merged = term if merged is None else merged + term
    x = x_ref[...] + _dot(merged.astype(bf16), wo_ref[...])
    xo_ref[...] = x
    h_ref[...] = (x * lax.rsqrt(jnp.mean(x * x, axis=-1, keepdims=True) + EPS) * gn_ref[...]).astype(bf16)


def _merge(ya, yb, yc, gl, gate_b, x2d, wa, wb, wc, wo, gn, tm):
    m = x2d.shape[0]
    row = lambda wd: pl.BlockSpec((tm, wd), lambda i: (i, 0))
    full = lambda a: pl.BlockSpec(a.shape, lambda i: (0, 0))
    return pl.pallas_call(
        _merge_kernel,
        grid=(m // tm,),
        in_specs=[row(FOX_WIDTH), row(HGRN_PAD_WIDTH), row(CONV_CH), row(GL_W), full(gate_b),
                  row(D_MODEL), full(wa), full(wb), full(wc), full(wo), full(gn)],
        out_specs=[row(D_MODEL), row(D_MODEL)],
        out_shape=[jax.ShapeDtypeStruct((m, D_MODEL), f32), jax.ShapeDtypeStruct((m, D_MODEL), bf16)],
        compiler_params=_cparams(("parallel",)),
        name="merge",
    )(ya, yb, yc, gl, gate_b, x2d, wa, wb, wc, wo, gn)


_FFN_HALO = 2 * SUBLANE
_FFN_COLS = 256


def _ffn_kernel(h_ref, hprev_ref, x_ref, wup_ref, cw_ref, cb_ref, wd_ref, o_ref, gbuf_ref, act_ref,
                *, tm, blocks_per_seq):
    i = pl.program_id(0)
    seq_start = (i % blocks_per_seq) == 0
    h_ext = jnp.concatenate([hprev_ref[...], h_ref[...]], axis=0)
    row = lax.broadcasted_iota(jnp.int32, (tm + _FFN_HALO, _FFN_COLS), 0)
    keep = jnp.logical_or(row >= _FFN_HALO, jnp.logical_not(seq_start))
    for n, c0 in enumerate(range(0, D_FF, _FFN_COLS)):
        cols = slice(c0, c0 + _FFN_COLS)
        gbuf = gbuf_ref.at[n % 2]
        g_ext = _dot(h_ext, wup_ref[:, cols])
        gbuf[...] = jnp.where(keep, g_ext, 0.0)
        conv = cb_ref[:, cols] + cw_ref[FFN_CONV_WIDTH - 1:FFN_CONV_WIDTH, cols] * g_ext[_FFN_HALO:]
        for k in range(FFN_CONV_WIDTH - 1):
            off = _FFN_HALO - (FFN_CONV_WIDTH - 1) + k
            conv = conv + cw_ref[k:k + 1, cols] * gbuf[off:off + tm, :]
        val = _dot(h_ref[...], wup_ref[:, D_FF + c0:D_FF + c0 + _FFN_COLS])
        act_ref[:, cols] = (conv * _sigmoid(conv) * val).astype(bf16)
    o_ref[...] = x_ref[...] + _dot(act_ref[...], wd_ref[...])


def _ffn(h2, x2d, wup, cw, cb, wd, tm, seq):
    m = x2d.shape[0]
    kern = functools.partial(_ffn_kernel, tm=tm, blocks_per_seq=seq // tm)
    halo_blocks = tm // _FFN_HALO
    resident = lambda a: pl.BlockSpec(a.shape, lambda i: (0, 0), pipeline_mode=pl.Buffered(1))
    return pl.pallas_call(
        kern,
        grid=(m // tm,),
        in_specs=[
            pl.BlockSpec((tm, D_MODEL), lambda i: (i, 0)),
            pl.BlockSpec((_FFN_HALO, D_MODEL), lambda i: (jnp.maximum(i * halo_blocks - 1, 0), 0)),
            pl.BlockSpec((tm, D_MODEL), lambda i: (i, 0)),
            resident(wup), resident(cw), resident(cb), resident(wd),
        ],
        out_specs=pl.BlockSpec((tm, D_MODEL), lambda i: (i, 0)),
        out_shape=jax.ShapeDtypeStruct((m, D_MODEL), f32),
        scratch_shapes=[pltpu.VMEM((2, tm + _FFN_HALO, _FFN_COLS), f32), pltpu.VMEM((tm, D_FF), bf16)],
        compiler_params=_cparams(("parallel",)),
        name="ffn",
    )(h2, h2, x2d, wup, cw, cb, wd)


def _pad_heads(w, axis):
    shape = list(w.shape)
    shape[axis:axis + 1] = [HGRN_HEADS, HGRN_VAL_DIM]
    w = w.reshape(shape)
    pad = [(0, 0)] * len(shape)
    pad[axis + 1] = (0, HGRN_VAL_PAD - HGRN_VAL_DIM)
    w = jnp.pad(w, pad)
    shape[axis:axis + 2] = [HGRN_PAD_WIDTH]
    return w.reshape(shape)


def _block(total, want):
    return min(total, want)


@jax.jit
def _forward(x, norm_mix_g, w_in, fox_forget_b, fox_q_norm_g, fox_k_norm_g, hgrn_lb_logits,
             hgrn_out_norm_g, conv_dw_w, conv_dw_b, conv_norm_g, conv_norm_b, gate_b, w_branch,
             w_out, norm_ffn_g, w_up, ffn_dw_w, ffn_dw_b, w_down):
    b, s, d = x.shape
    depth = w_in.shape[0]
    m = b * s

    tm_in = _block(m, 256)
    ts_prep = _block(s, 512)
    tq = _block(s, 1024)
    tk = 256
    tb_hgrn = _block(s, 512)
    ts_conv = _block(s, 256)
    tm_merge = _block(m, 512)
    tm_ffn = _block(s, 512)

    lb_all = jnp.cumsum(jax.nn.softmax(hgrn_lb_logits.astype(f32), axis=0), axis=0)
    lb_all = jnp.maximum(lb_all - lb_all[0:1], 0.0)
    log_lb = jnp.log(lb_all)
    log_1m_lb = jnp.log1p(-lb_all)

    tri = jnp.asarray(np.tril(np.ones((ts_prep, ts_prep), np.float32)), bf16)
    cmat = jnp.asarray(_hgrn_prefix_matrix(), bf16)
    o_a = FOX_WIDTH
    o_b = FOX_WIDTH + HGRN_VAL_WIDTH

    w_in_cat = _w_in_prep(w_in)
    w_up_bf = w_up.astype(bf16)
    w_down_bf = w_down.astype(bf16)

    x2d = x.reshape(m, d)
    for l in range(depth):
        qkv, ff, hqig, hf, cab, gl = _in_proj(x2d, norm_mix_g[l][None, :], w_in_cat, l, tm_in)

        fb = jnp.pad(fox_forget_b[l], (0, LANE - FOX_HEADS))[None, :]
        gq = jnp.tile(fox_q_norm_g[l], FOX_HEADS)[None, :]
        gk = jnp.tile(fox_k_norm_g[l], FOX_HEADS)[None, :]
        qa, ka, vt = _fox_prep(qkv.reshape(b, s, QKV_W), ff.reshape(b, s, FF_W), fb, gq, gk, tri, ts_prep)
        ya = _fox_attn(qa, ka, vt, tq, tk)

        gn = jnp.tile(jnp.pad(hgrn_out_norm_g[l], (0, HGRN_VAL_PAD - HGRN_VAL_DIM)), HGRN_HEADS)[None, :]
        yb = _hgrn(hqig.reshape(b, s, HQIG_W), hf.reshape(b, s, HF_W), log_lb[l][None, :],
                   log_1m_lb[l][None, :], gn, cmat, tb_hgrn)

        yc = _conv(cab.reshape(b, s, CAB_W), conv_dw_w[l], conv_dw_b[l][None, :],
                   conv_norm_g[l][None, :], conv_norm_b[l][None, :], ts_conv)

        wb = w_branch[l]
        x2d, h2 = _merge(ya.reshape(m, FOX_WIDTH), yb.reshape(m, HGRN_PAD_WIDTH), yc.reshape(m, CONV_CH),
                         gl, gate_b[l][None, :], x2d,
                         wb[:o_a].astype(bf16), _pad_heads(wb[o_a:o_b], 0).astype(bf16),
                         wb[o_b:].astype(bf16), w_out[l].astype(bf16), norm_ffn_g[l][None, :], tm_merge)

        x2d = _ffn(h2, x2d, w_up_bf[l], ffn_dw_w[l], ffn_dw_b[l][None, :], w_down_bf[l], tm_ffn, s)
    return x2d.reshape(b, s, d)


def kernel(x, norm_mix_g, w_in, fox_forget_b, fox_q_norm_g, fox_k_norm_g, hgrn_lb_logits,
           hgrn_out_norm_g, conv_dw_w, conv_dw_b, conv_norm_g, conv_norm_b, gate_b, w_branch, w_out,
           norm_ffn_g, w_up, ffn_dw_w, ffn_dw_b, w_down):
    return _forward(x, norm_mix_g, w_in, fox_forget_b, fox_q_norm_g, fox_k_norm_g, hgrn_lb_logits,
                    hgrn_out_norm_g, conv_dw_w, conv_dw_b, conv_norm_g, conv_norm_b, gate_b, w_branch,
                    w_out, norm_ffn_g, w_up, ffn_dw_w, ffn_dw_b, w_down)
```

```python
import functools

import numpy as np
import jax
import jax.numpy as jnp
from jax import lax
from jax.experimental import pallas as pl
from jax.experimental.pallas import tpu as pltpu

D_MODEL = 1024
CHUNK = 64
FOX_HEADS = 6
FOX_HEAD_DIM = 64
FOX_WIDTH = FOX_HEADS * FOX_HEAD_DIM
HGRN_HEADS = 4
HGRN_KEY_DIM = 128
HGRN_VAL_DIM = 96
HGRN_KEY_WIDTH = HGRN_HEADS * HGRN_KEY_DIM
HGRN_VAL_WIDTH = HGRN_HEADS * HGRN_VAL_DIM
CONV_CH = 256
CONV_WIDTH = 31
N_BRANCH = 3
D_FF = 2816
FFN_CONV_WIDTH = 3
EPS = 1e-6

LANE = 128
SUBLANE = 8
VMEM_LIMIT_BYTES = 56 * 1024 * 1024

HGRN_VAL_PAD = LANE
HGRN_PAD_WIDTH = HGRN_HEADS * HGRN_VAL_PAD
FOX_AUG = LANE
FOX_VT_ROWS = 80
FOX_PAIRS = FOX_HEADS // 2
NEG_BIG = -1e30
LOG2_E = 1.4426950408889634

QKV_W = 3 * FOX_WIDTH
FF_W = LANE
HQIG_W = HGRN_KEY_WIDTH + 2 * HGRN_PAD_WIDTH
HF_W = HGRN_KEY_WIDTH
CAB_W = 2 * CONV_CH
GL_W = N_BRANCH * D_MODEL
IN_PROJ_W = QKV_W + FF_W + HQIG_W + HF_W + CAB_W + GL_W

f32 = jnp.float32
bf16 = jnp.bfloat16


def _cparams(semantics):
    return pltpu.CompilerParams(dimension_semantics=semantics, vmem_limit_bytes=VMEM_LIMIT_BYTES)


def _log_sigmoid(x):
    return jnp.minimum(x, 0.0) - jnp.log1p(jnp.exp(-jnp.abs(x)))


def _sigmoid(x):
    return 1.0 / (1.0 + jnp.exp(-x))


def _dot(a, b):
    return jnp.dot(a, b, preferred_element_type=f32)


def _dot_nt(a, b):
    return lax.dot_general(a, b, (((1,), (1,)), ((), ())), preferred_element_type=f32)


def _dot_tn(a, b):
    return lax.dot_general(a, b, (((0,), (0,)), ((), ())), preferred_element_type=f32)


def _split_bf16(x, n):
    parts = []
    r = x
    for _ in range(n):
        p = r.astype(bf16)
        parts.append(p)
        r = r - p.astype(f32)
    return parts


_IN_SIZES = (FOX_WIDTH, FOX_WIDTH, FOX_WIDTH, FOX_HEADS, HGRN_KEY_WIDTH, HGRN_KEY_WIDTH,
             HGRN_VAL_WIDTH, HGRN_VAL_WIDTH, CONV_CH, CONV_CH, N_BRANCH * D_MODEL)
_IN_OFFS = tuple(int(v) for v in np.cumsum((0,) + _IN_SIZES))
_W_PREP_ROWS = 128


def _w_in_prep_kernel(w_ref, o_ref):
    o_fq, _, _, o_ff, o_hq, o_hf, o_hi, o_hg, o_ca, _, o_gl, _ = _IN_OFFS
    rows = w_ref.shape[1]

    def src(a, width):
        return w_ref[0, :, a:a + width]

    def padded_heads(a):
        pieces = []
        for h in range(HGRN_HEADS):
            pieces.append(src(a + h * HGRN_VAL_DIM, HGRN_VAL_DIM))
            pieces.append(jnp.zeros((rows, HGRN_VAL_PAD - HGRN_VAL_DIM), f32))
        return jnp.concatenate(pieces, axis=1)

    segments = [
        src(o_fq, QKV_W),
        jnp.concatenate([src(o_ff, FOX_HEADS), jnp.zeros((rows, FF_W - FOX_HEADS), f32)], axis=1),
        src(o_hq, HGRN_KEY_WIDTH),
        padded_heads(o_hi),
        padded_heads(o_hg),
        src(o_hf, HGRN_KEY_WIDTH),
        src(o_ca, CAB_W),
        src(o_gl, GL_W),
    ]
    c0 = 0
    for seg in segments:
        o_ref[0, :, c0:c0 + seg.shape[1]] = seg.astype(bf16)
        c0 += seg.shape[1]


def _w_in_prep(w_in):
    depth, d, n = w_in.shape
    return pl.pallas_call(
        _w_in_prep_kernel,
        grid=(depth, d // _W_PREP_ROWS),
        in_specs=[pl.BlockSpec((1, _W_PREP_ROWS, n), lambda l, i: (l, i, 0))],
        out_specs=pl.BlockSpec((1, _W_PREP_ROWS, IN_PROJ_W), lambda l, i: (l, i, 0)),
        out_shape=jax.ShapeDtypeStruct((depth, d, IN_PROJ_W), bf16),
        compiler_params=_cparams(("parallel", "parallel")),
        name="w_in_prep",
    )(w_in)


def _in_proj_kernel(x_ref, g_ref, w_ref, qkv_ref, ff_ref, hqig_ref, hf_ref, cab_ref, gl_ref):
    x = x_ref[...]
    h = (x * lax.rsqrt(jnp.mean(x * x, axis=-1, keepdims=True) + EPS) * g_ref[...]).astype(bf16)

    c0 = 0
    z = _dot(h, w_ref[:, c0:c0 + QKV_W + FF_W])
    qkv_ref[...] = z[:, :QKV_W].astype(bf16)
    ff_ref[...] = z[:, QKV_W:]
    c0 += QKV_W + FF_W
    hqig_ref[...] = _dot(h, w_ref[:, c0:c0 + HQIG_W]).astype(bf16)
    c0 += HQIG_W
    hf_ref[...] = _dot(h, w_ref[:, c0:c0 + HF_W])
    c0 += HF_W
    cab_ref[...] = _dot(h, w_ref[:, c0:c0 + CAB_W]).astype(bf16)
    c0 += CAB_W
    half = GL_W // 2
    for a in (0, half):
        gl_ref[:, a:a + half] = _dot(h, w_ref[:, c0 + a:c0 + a + half]).astype(bf16)


def _in_proj(x2d, g, w_all, layer, tm):
    m = x2d.shape[0]
    widths = (QKV_W, FF_W, HQIG_W, HF_W, CAB_W, GL_W)
    dtypes = (bf16, f32, bf16, f32, bf16, bf16)
    return pl.pallas_call(
        _in_proj_kernel,
        grid=(m // tm,),
        in_specs=[
            pl.BlockSpec((tm, D_MODEL), lambda i: (i, 0)),
            pl.BlockSpec((1, D_MODEL), lambda i: (0, 0)),
            pl.BlockSpec((None, D_MODEL, IN_PROJ_W), lambda i: (layer, 0, 0), pipeline_mode=pl.Buffered(1)),
        ],
        out_specs=[pl.BlockSpec((tm, wd), lambda i: (i, 0)) for wd in widths],
        out_shape=[jax.ShapeDtypeStruct((m, wd), dt) for wd, dt in zip(widths, dtypes)],
        compiler_params=_cparams(("parallel",)),
        name="in_proj",
    )(x2d, g, w_all)


def _fox_prep_kernel(qkv_ref, ff_ref, fb_ref, gq_ref, gk_ref, tri_ref,
                     q_out, k_out, vt_out, carry_ref, *, ts):
    @pl.when(pl.program_id(1) == 0)
    def _():
        carry_ref[...] = jnp.zeros_like(carry_ref)

    lane = lax.broadcasted_iota(jnp.int32, (ts, LANE), 1)
    low = lane < FOX_HEAD_DIM

    lf = _log_sigmoid(ff_ref[0] + fb_ref[...])
    parts = jnp.concatenate(_split_bf16(lf, 3), axis=1)
    cs = _dot(tri_ref[...], parts)
    c = (cs[:, :LANE] + carry_ref[...]) + cs[:, LANE:2 * LANE] + cs[:, 2 * LANE:]
    carry_ref[...] = c[ts - 1:ts, :]

    def normed(col0, gain_ref, scale):
        blocks = []
        for j in range(FOX_PAIRS):
            blk = qkv_ref[0, :, col0 + j * LANE:col0 + (j + 1) * LANE].astype(f32)
            sq = blk * blk
            s_lo = jnp.sum(jnp.where(low, sq, 0.0), axis=1, keepdims=True)
            s_hi = jnp.sum(jnp.where(low, 0.0, sq), axis=1, keepdims=True)
            rs = jnp.where(low, lax.rsqrt(s_lo * (1.0 / FOX_HEAD_DIM) + EPS),
                           lax.rsqrt(s_hi * (1.0 / FOX_HEAD_DIM) + EPS))
            blocks.append(blk * rs * (gain_ref[:, j * LANE:(j + 1) * LANE] * scale))
        return blocks

    qn = normed(0, gq_ref, FOX_HEAD_DIM ** -0.5 * LOG2_E)
    kn = normed(FOX_WIDTH, gk_ref, 1.0)
    c2 = c * LOG2_E

    for h in range(FOX_HEADS):
        cb = jnp.broadcast_to(c2[:, h:h + 1], (ts, LANE))
        c3 = [p.astype(f32) for p in _split_bf16(cb, 3)]

        def head_part(blocks):
            blk = blocks[h // 2]
            if h % 2 == 1:
                blk = pltpu.roll(blk, FOX_HEAD_DIM, axis=1)
            return jnp.where(low, blk, 0.0)

        qa = head_part(qn)
        ka = head_part(kn)
        for i in range(3):
            qa = jnp.where(lane == FOX_HEAD_DIM + i, c3[i], qa)
            qa = jnp.where(lane == FOX_HEAD_DIM + 3 + i, 1.0, qa)
            ka = jnp.where(lane == FOX_HEAD_DIM + i, 1.0, ka)
            ka = jnp.where(lane == FOX_HEAD_DIM + 3 + i, -c3[i], ka)
        q_out[0, h] = qa.astype(bf16)
        k_out[0, h] = ka.astype(bf16)

    vt = qkv_ref[0, :, 2 * FOX_WIDTH:3 * FOX_WIDTH].astype(f32).T
    tail_row = lax.broadcasted_iota(jnp.int32, (FOX_VT_ROWS - FOX_HEAD_DIM, ts), 0)
    tail = jnp.where(tail_row == 0, 1.0, 0.0).astype(bf16)
    for h in range(FOX_HEADS):
        vt_out[0, h, 0:FOX_HEAD_DIM, :] = vt[h * FOX_HEAD_DIM:(h + 1) * FOX_HEAD_DIM, :].astype(bf16)
        vt_out[0, h, FOX_HEAD_DIM:FOX_VT_ROWS, :] = tail


def _fox_prep(qkv, ff, fb, gq, gk, tri, ts):
    b, s, _ = qkv.shape
    kern = functools.partial(_fox_prep_kernel, ts=ts)
    return pl.pallas_call(
        kern,
        grid=(b, s // ts),
        in_specs=[
            pl.BlockSpec((1, ts, QKV_W), lambda i, j: (i, j, 0)),
            pl.BlockSpec((1, ts, FF_W), lambda i, j: (i, j, 0)),
            pl.BlockSpec((1, LANE), lambda i, j: (0, 0)),
            pl.BlockSpec((1, FOX_WIDTH), lambda i, j: (0, 0)),
            pl.BlockSpec((1, FOX_WIDTH), lambda i, j: (0, 0)),
            pl.BlockSpec((ts, ts), lambda i, j: (0, 0)),
        ],
        out_specs=[
            pl.BlockSpec((1, FOX_HEADS, ts, FOX_AUG), lambda i, j: (i, 0, j, 0)),
            pl.BlockSpec((1, FOX_HEADS, ts, FOX_AUG), lambda i, j: (i, 0, j, 0)),
            pl.BlockSpec((1, FOX_HEADS, FOX_VT_ROWS, ts), lambda i, j: (i, 0, 0, j)),
        ],
        out_shape=[
            jax.ShapeDtypeStruct((b, FOX_HEADS, s, FOX_AUG), bf16),
            jax.ShapeDtypeStruct((b, FOX_HEADS, s, FOX_AUG), bf16),
            jax.ShapeDtypeStruct((b, FOX_HEADS, FOX_VT_ROWS, s), bf16),
        ],
        scratch_shapes=[pltpu.VMEM((1, LANE), f32)],
        compiler_params=_cparams(("parallel", "arbitrary")),
        name="fox_prep",
    )(qkv, ff, fb, gq, gk, tri)


def _fox_attn_kernel(q_ref, k_ref, vt_ref, o_ref, m_ref, acc_ref, s_ref, *, tq, tk):
    qi = pl.program_id(2)
    nsub = tq // tk
    key_idx = lax.broadcasted_iota(jnp.int32, (tk, tk), 0)
    qry_idx = lax.broadcasted_iota(jnp.int32, (tk, tk), 1)
    causal = key_idx <= qry_idx
    m_ref[...] = jnp.full(m_ref.shape, NEG_BIG, f32)
    acc_ref[...] = jnp.zeros(acc_ref.shape, f32)

    def scores(hh, kb, lo):
        start = pl.multiple_of(kb * tk, tk)
        return _dot_nt(k_ref[0, hh, pl.ds(start, tk), :], q_ref[0, hh, lo:tq, :])

    def update(hh, kb, lo, st, diagonal):
        start = pl.multiple_of(kb * tk, tk)
        if diagonal:
            head = jnp.where(causal, st[:, :tk], NEG_BIG)
            st = head if lo + tk == tq else jnp.concatenate([head, st[:, tk:]], axis=1)
        m_old = m_ref[hh, :, lo:tq]
        m_new = jnp.maximum(m_old, jnp.max(st, axis=0, keepdims=True))
        p = jnp.exp2(st - m_new).astype(bf16)
        pv = _dot(vt_ref[0, hh, :, pl.ds(start, tk)], p)
        acc_ref[hh, :, lo:tq] = jnp.exp2(m_old - m_new) * acc_ref[hh, :, lo:tq] + pv
        m_ref[hh, :, lo:tq] = m_new

    n_main = qi * nsub
    for hh in range(2):
        s_ref[0, hh] = scores(hh, 0, 0)

    def body(it, carry):
        kb = it * 2
        for slot in range(2):
            for hh in range(2):
                s_ref[1 - slot, hh] = scores(hh, kb + slot + 1, 0)
                update(hh, kb + slot, 0, s_ref[slot, hh], False)
        return carry

    lax.fori_loop(0, n_main // 2, body, 0)
    for e in range(nsub):
        for hh in range(2):
            if e + 1 < nsub:
                s_ref[(e + 1) % 2, hh, :, (e + 1) * tk:tq] = scores(hh, n_main + e + 1, (e + 1) * tk)
            update(hh, n_main + e, e * tk, s_ref[e % 2, hh, :, e * tk:tq], True)

    outs = [acc_ref[hh, 0:FOX_HEAD_DIM, :] / acc_ref[hh, FOX_HEAD_DIM:FOX_HEAD_DIM + 1, :] for hh in range(2)]
    o_ref[0] = jnp.concatenate(outs, axis=0).T.astype(bf16)


def _fox_attn(qa, ka, vt, tq, tk):
    b, _, s, _ = qa.shape
    kern = functools.partial(_fox_attn_kernel, tq=tq, tk=tk)
    return pl.pallas_call(
        kern,
        grid=(b, FOX_PAIRS, s // tq),
        in_specs=[
            pl.BlockSpec((1, 2, tq, FOX_AUG), lambda i, p, j: (i, p, j, 0)),
            pl.BlockSpec((1, 2, s, FOX_AUG), lambda i, p, j: (i, p, 0, 0)),
            pl.BlockSpec((1, 2, FOX_VT_ROWS, s), lambda i, p, j: (i, p, 0, 0)),
        ],
        out_specs=pl.BlockSpec((1, tq, LANE), lambda i, p, j: (i, j, p)),
        out_shape=jax.ShapeDtypeStruct((b, s, FOX_WIDTH), bf16),
        scratch_shapes=[pltpu.VMEM((2, 1, tq), f32), pltpu.VMEM((2, FOX_VT_ROWS, tq), f32),
                        pltpu.VMEM((2, 2, tk, tq), f32)],
        compiler_params=_cparams(("parallel", "parallel", "arbitrary")),
        name="fox_attn",
    )(qa, ka, vt)


_HGRN_HALVES = (1, 2, 4, 8, 16, 32)
_HGRN_MXU_LEVELS = 3


def _hgrn_prefix_matrix():
    mats = []
    for hsz in _HGRN_HALVES[:_HGRN_MXU_LEVELS]:
        m = np.zeros((CHUNK, CHUNK), np.float32)
        for t in range(CHUNK):
            bd = (t // (2 * hsz)) * 2 * hsz + hsz - 1
            if t % (2 * hsz) >= hsz:
                m[t, bd + 1:t + 1] = 1.0
            else:
                m[t, t + 1:bd + 1] = 1.0
        mats.append(m)
    mats.append(np.tril(np.ones((CHUNK, CHUNK), np.float32)))
    return np.concatenate(mats, axis=0)


def _hgrn_kernel(hqig_ref, hf_ref, llb_ref, l1m_ref, gn_ref, cmat_ref, y_ref, st_ref, *, n_chunks):
    @pl.when(pl.program_id(1) == 0)
    def _():
        st_ref[...] = jnp.zeros_like(st_ref)

    t_i = lax.broadcasted_iota(jnp.int32, (CHUNK, CHUNK), 0)
    s_i = lax.broadcasted_iota(jnp.int32, (CHUNK, CHUNK), 1)
    masks = []
    for hsz in _HGRN_HALVES:
        same = (t_i // (2 * hsz)) == (s_i // (2 * hsz))
        masks.append(same & ((t_i % (2 * hsz)) >= hsz) & ((s_i % (2 * hsz)) < hsz))
    cmat = cmat_ref[...]
    kw, vw = HGRN_KEY_WIDTH, HGRN_PAD_WIDTH

    def chunk(c, carry):
        r0 = pl.multiple_of(c * CHUNK, CHUNK)
        rows = pl.ds(r0, CHUNK)
        for h in range(HGRN_HEADS):
            kcols = slice(h * HGRN_KEY_DIM, (h + 1) * HGRN_KEY_DIM)
            q = hqig_ref[0, rows, kcols].astype(f32)
            v = hqig_ref[0, rows, kw + h * HGRN_VAL_PAD:kw + (h + 1) * HGRN_VAL_PAD]
            g = hqig_ref[0, rows, kw + vw + h * HGRN_VAL_PAD:kw + vw + (h + 1) * HGRN_VAL_PAD].astype(f32)
            x = hf_ref[0, rows, kcols]

            ls = _log_sigmoid(x)
            a = llb_ref[:, kcols]
            bb = l1m_ref[:, kcols] + ls
            logf = jnp.maximum(a, bb) + jnp.log1p(jnp.exp(-jnp.abs(a - bb)))
            kk = jnp.exp(l1m_ref[:, kcols] + ls - x)

            lh = jnp.concatenate(_split_bf16(logf, 2), axis=1)
            e = _dot(cmat, lh)
            e = e[:, :HGRN_KEY_DIM] + e[:, HGRN_KEY_DIM:]
            nl = _HGRN_MXU_LEVELS
            b = e[nl * CHUNK:(nl + 1) * CHUNK]
            decays = [jnp.exp(e[l * CHUNK:(l + 1) * CHUNK]) for l in range(nl)]
            for hsz in _HGRN_HALVES[nl:]:
                bc = jnp.concatenate(
                    [jnp.broadcast_to(b[blk + hsz - 1:blk + hsz, :], (2 * hsz, HGRN_KEY_DIM))
                     for blk in range(0, CHUNK, 2 * hsz)], axis=0)
                decays.append(jnp.exp(-jnp.abs(b - bc)))

            attn = jnp.zeros((CHUNK, CHUNK), f32)
            for mask, dec in zip(masks, decays):
                sc = _dot_nt((q * dec).astype(bf16), (kk * dec).astype(bf16))
                attn = attn + jnp.where(mask, sc, 0.0)
            diag = jnp.sum(q * kk, axis=1, keepdims=True)

            state_t = st_ref[h]
            o = (_dot(attn.astype(bf16), v) + diag * v.astype(f32)
                 + _dot_nt((q * jnp.exp(b)).astype(bf16), state_t.astype(bf16)))
            b_last = b[CHUNK - 1:CHUNK, :]
            kd = (kk * jnp.exp(b_last - b)).astype(bf16)
            st_ref[h] = jnp.exp(b_last) * state_t + _dot_tn(v, kd)

            ms = jnp.sum(o * o, axis=1, keepdims=True) * (1.0 / HGRN_VAL_DIM)
            vcols = slice(h * HGRN_VAL_PAD, (h + 1) * HGRN_VAL_PAD)
            y = o * lax.rsqrt(ms + EPS) * gn_ref[:, vcols] * (g * _sigmoid(g))
            y_ref[0, rows, vcols] = y.astype(bf16)
        return carry

    lax.fori_loop(0, n_chunks, chunk, 0, unroll=2)


def _hgrn(hqig, hf, llb, l1m, gn, cmat, tb):
    b, s, _ = hqig.shape
    kern = functools.partial(_hgrn_kernel, n_chunks=tb // CHUNK)
    return pl.pallas_call(
        kern,
        grid=(b, s // tb),
        in_specs=[
            pl.BlockSpec((1, tb, HQIG_W), lambda i, j: (i, j, 0)),
            pl.BlockSpec((1, tb, HF_W), lambda i, j: (i, j, 0)),
            pl.BlockSpec((1, HGRN_KEY_WIDTH), lambda i, j: (0, 0)),
            pl.BlockSpec((1, HGRN_KEY_WIDTH), lambda i, j: (0, 0)),
            pl.BlockSpec((1, HGRN_PAD_WIDTH), lambda i, j: (0, 0)),
            pl.BlockSpec(cmat.shape, lambda i, j: (0, 0)),
        ],
        out_specs=pl.BlockSpec((1, tb, HGRN_PAD_WIDTH), lambda i, j: (i, j, 0)),
        out_shape=jax.ShapeDtypeStruct((b, s, HGRN_PAD_WIDTH), bf16),
        scratch_shapes=[pltpu.VMEM((HGRN_HEADS, HGRN_VAL_PAD, HGRN_KEY_DIM), f32)],
        compiler_params=_cparams(("parallel", "arbitrary")),
        name="hgrn",
    )(hqig, hf, llb, l1m, gn, cmat)


_CONV_HIST = 32
_CONV_ROWS = 64


def _conv_kernel(cab_ref, w_ref, b_ref, g_ref, beta_ref, y_ref, ubuf_ref, *, ts):
    @pl.when(pl.program_id(1) == 0)
    def _():
        ubuf_ref[0:_CONV_HIST, :] = jnp.zeros((_CONV_HIST, CONV_CH), f32)

    ca = cab_ref[0, :, 0:CONV_CH].astype(f32)
    cb = cab_ref[0, :, CONV_CH:2 * CONV_CH].astype(f32)
    ubuf_ref[_CONV_HIST:_CONV_HIST + ts, :] = ca * _sigmoid(cb)

    first = _CONV_HIST - (CONV_WIDTH - 1)
    for r0 in range(0, ts, _CONV_ROWS):
        acc = jnp.broadcast_to(b_ref[...], (_CONV_ROWS, CONV_CH))
        for j in range(CONV_WIDTH):
            acc = acc + w_ref[j:j + 1, :] * ubuf_ref[r0 + first + j:r0 + first + j + _CONV_ROWS, :]
        mu = jnp.mean(acc, axis=-1, keepdims=True)
        xc = acc - mu
        yn = xc * lax.rsqrt(jnp.mean(xc * xc, axis=-1, keepdims=True) + EPS)
        yn = yn * g_ref[...] + beta_ref[...]
        y_ref[0, r0:r0 + _CONV_ROWS, :] = (yn * _sigmoid(yn)).astype(bf16)

    ubuf_ref[0:_CONV_HIST, :] = ubuf_ref[ts:ts + _CONV_HIST, :]


def _conv(cab, w, bias, g, beta, ts):
    b, s, _ = cab.shape
    kern = functools.partial(_conv_kernel, ts=ts)
    vec = pl.BlockSpec((1, CONV_CH), lambda i, j: (0, 0))
    return pl.pallas_call(
        kern,
        grid=(b, s // ts),
        in_specs=[
            pl.BlockSpec((1, ts, CAB_W), lambda i, j: (i, j, 0)),
            pl.BlockSpec((CONV_WIDTH, CONV_CH), lambda i, j: (0, 0)),
            vec, vec, vec,
        ],
        out_specs=pl.BlockSpec((1, ts, CONV_CH), lambda i, j: (i, j, 0)),
        out_shape=jax.ShapeDtypeStruct((b, s, CONV_CH), bf16),
        scratch_shapes=[pltpu.VMEM((ts + _CONV_HIST, CONV_CH), f32)],
        compiler_params=_cparams(("parallel", "arbitrary")),
        name="conv",
    )(cab, w, bias, g, beta)


def _merge_kernel(ya_ref, yb_ref, yc_ref, gl_ref, gb_ref, x_ref, wa_ref, wb_ref, wc_ref, wo_ref,
                  gn_ref, xo_ref, h_ref):
    merged = None
    for i, (y_ref, w_ref) in enumerate(((ya_ref, wa_ref), (yb_ref, wb_ref), (yc_ref, wc_ref))):
        cols = slice(i * D_MODEL, (i + 1) * D_MODEL)
        gate = _sigmoid(gl_ref[:, cols].astype(f32) + gb_ref[:, cols])
        term = gate * _dot(y_ref[...], w_ref[...])
        merged = term if merged is None else merged + term
    x = x_ref[...] + _dot(merged.astype(bf16), wo_ref[...])
    xo_ref[...] = x
    h_ref[...] = (x * lax.rsqrt(jnp.mean(x * x, axis=-1, keepdims=True) + EPS) * gn_ref[...]).astype(bf16)


def _merge(ya, yb, yc, gl, gate_b, x2d, wa, wb, wc, wo, gn, tm):
    m = x2d.shape[0]
    row = lambda wd: pl.BlockSpec((tm, wd), lambda i: (i, 0))
    full = lambda a: pl.BlockSpec(a.shape, lambda i: (0, 0))
    return pl.pallas_call(
        _merge_kernel,
        grid=(m // tm,),
        in_specs=[row(FOX_WIDTH), row(HGRN_PAD_WIDTH), row(CONV_CH), row(GL_W), full(gate_b),
                  row(D_MODEL), full(wa), full(wb), full(wc), full(wo), full(gn)],
        out_specs=[row(D_MODEL), row(D_MODEL)],
        out_shape=[jax.ShapeDtypeStruct((m, D_MODEL), f32), jax.ShapeDtypeStruct((m, D_MODEL), bf16)],
        compiler_params=_cparams(("parallel",)),
        name="merge",
    )(ya, yb, yc, gl, gate_b, x2d, wa, wb, wc, wo, gn)


_FFN_HALO = 2 * SUBLANE
_FFN_COLS = 256


def _ffn_kernel(h_ref, hprev_ref, x_ref, wup_ref, cw_ref, cb_ref, wd_ref, o_ref, gbuf_ref, act_ref,
                *, tm, blocks_per_seq):
    i = pl.program_id(0)
    seq_start = (i % blocks_per_seq) == 0
    h_ext = jnp.concatenate([hprev_ref[...], h_ref[...]], axis=0)
    row = lax.broadcasted_iota(jnp.int32, (tm + _FFN_HALO, _FFN_COLS), 0)
    keep = jnp.logical_or(row >= _FFN_HALO, jnp.logical_not(seq_start))
    for n, c0 in enumerate(range(0, D_FF, _FFN_COLS)):
        cols = slice(c0, c0 + _FFN_COLS)
        gbuf = gbuf_ref.at[n % 2]
        g_ext = _dot(h_ext, wup_ref[:, cols])
        gbuf[...] = jnp.where(keep, g_ext, 0.0)
        conv = cb_ref[:, cols] + cw_ref[FFN_CONV_WIDTH - 1:FFN_CONV_WIDTH, cols] * g_ext[_FFN_HALO:]
        for k in range(FFN_CONV_WIDTH - 1):
            off = _FFN_HALO - (FFN_CONV_WIDTH - 1) + k
            conv = conv + cw_ref[k:k + 1, cols] * gbuf[off:off + tm, :]
        val = _dot(h_ref[...], wup_ref[:, D_FF + c0:D_FF + c0 + _FFN_COLS])
        act_ref[:, cols] = (conv * _sigmoid(conv) * val).astype(bf16)
    o_ref[...] = x_ref[...] + _dot(act_ref[...], wd_ref[...])


def _ffn(h2, x2d, wup, cw, cb, wd, tm, seq):
    m = x2d.shape[0]
    kern = functools.partial(_ffn_kernel, tm=tm, blocks_per_seq=seq // tm)
    halo_blocks = tm // _FFN_HALO
    resident = lambda a: pl.BlockSpec(a.shape, lambda i: (0, 0), pipeline_mode=pl.Buffered(1))
    return pl.pallas_call(
        kern,
        grid=(m // tm,),
        in_specs=[
            pl.BlockSpec((tm, D_MODEL), lambda i: (i, 0)),
            pl.BlockSpec((_FFN_HALO, D_MODEL), lambda i: (jnp.maximum(i * halo_blocks - 1, 0), 0)),
            pl.BlockSpec((tm, D_MODEL), lambda i: (i, 0)),
            resident(wup), resident(cw), resident(cb), resident(wd),
        ],
        out_specs=pl.BlockSpec((tm, D_MODEL), lambda i: (i, 0)),
        out_shape=jax.ShapeDtypeStruct((m, D_MODEL), f32),
        scratch_shapes=[pltpu.VMEM((2, tm + _FFN_HALO, _FFN_COLS), f32), pltpu.VMEM((tm, D_FF), bf16)],
        compiler_params=_cparams(("parallel",)),
        name="ffn",
    )(h2, h2, x2d, wup, cw, cb, wd)


def _pad_heads(w, axis):
    shape = list(w.shape)
    shape[axis:axis + 1] = [HGRN_HEADS, HGRN_VAL_DIM]
    w = w.reshape(shape)
    pad = [(0, 0)] * len(shape)
    pad[axis + 1] = (0, HGRN_VAL_PAD - HGRN_VAL_DIM)
    w = jnp.pad(w, pad)
    shape[axis:axis + 2] = [HGRN_PAD_WIDTH]
    return w.reshape(shape)


def _block(total, want):
    return min(total, want)


@jax.jit
def _forward(x, norm_mix_g, w_in, fox_forget_b, fox_q_norm_g, fox_k_norm_g, hgrn_lb_logits,
             hgrn_out_norm_g, conv_dw_w, conv_dw_b, conv_norm_g, conv_norm_b, gate_b, w_branch,
             w_out, norm_ffn_g, w_up, ffn_dw_w, ffn_dw_b, w_down):
    b, s, d = x.shape
    depth = w_in.shape[0]
    m = b * s

    tm_in = _block(m, 256)
    ts_prep = _block(s, 512)
    tq = _block(s, 1024)
    tk = 256
    tb_hgrn = _block(s, 512)
    ts_conv = _block(s, 256)
    tm_merge = _block(m, 512)
    tm_ffn = _block(s, 512)

    lb_all = jnp.cumsum(jax.nn.softmax(hgrn_lb_logits.astype(f32), axis=0), axis=0)
    lb_all = jnp.maximum(lb_all - lb_all[0:1], 0.0)
    log_lb = jnp.log(lb_all)
    log_1m_lb = jnp.log1p(-lb_all)

    tri = jnp.asarray(np.tril(np.ones((ts_prep, ts_prep), np.float32)), bf16)
    cmat = jnp.asarray(_hgrn_prefix_matrix(), bf16)
    o_a = FOX_WIDTH
    o_b = FOX_WIDTH + HGRN_VAL_WIDTH

    w_in_cat = _w_in_prep(w_in)
    w_up_bf = w_up.astype(bf16)
    w_down_bf = w_down.astype(bf16)

    x2d = x.reshape(m, d)
    for l in range(depth):
        qkv, ff, hqig, hf, cab, gl = _in_proj(x2d, norm_mix_g[l][None, :], w_in_cat, l, tm_in)

        fb = jnp.pad(fox_forget_b[l], (0, LANE - FOX_HEADS))[None, :]
        gq = jnp.tile(fox_q_norm_g[l], FOX_HEADS)[None, :]
        gk = jnp.tile(fox_k_norm_g[l], FOX_HEADS)[None, :]
        qa, ka, vt = _fox_prep(qkv.reshape(b, s, QKV_W), ff.reshape(b, s, FF_W), fb, gq, gk, tri, ts_prep)
        ya = _fox_attn(qa, ka, vt, tq, tk)

        gn = jnp.tile(jnp.pad(hgrn_out_norm_g[l], (0, HGRN_VAL_PAD - HGRN_VAL_DIM)), HGRN_HEADS)[None, :]
        yb = _hgrn(hqig.reshape(b, s, HQIG_W), hf.reshape(b, s, HF_W), log_lb[l][None, :],
                   log_1m_lb[l][None, :], gn, cmat, tb_hgrn)

        yc = _conv(cab.reshape(b, s, CAB_W), conv_dw_w[l], conv_dw_b[l][None, :],
                   conv_norm_g[l][None, :], conv_norm_b[l][None, :], ts_conv)

        wb = w_branch[l]
        x2d, h2 = _merge(ya.reshape(m, FOX_WIDTH), yb.reshape(m, HGRN_PAD_WIDTH), yc.reshape(m, CONV_CH),
                         gl, gate_b[l][None, :], x2d,
                         wb[:o_a].astype(bf16), _pad_heads(wb[o_a:o_b], 0).astype(bf16),
                         wb[o_b:].astype(bf16), w_out[l].astype(bf16), norm_ffn_g[l][None, :], tm_merge)

        x2d = _ffn(h2, x2d, w_up_bf[l], ffn_dw_w[l], ffn_dw_b[l][None, :], w_down_bf[l], tm_ffn, s)
    return x2d.reshape(b, s, d)


def kernel(x, norm_mix_g, w_in, fox_forget_b, fox_q_norm_g, fox_k_norm_g, hgrn_lb_logits,
           hgrn_out_norm_g, conv_dw_w, conv_dw_b, conv_norm_g, conv_norm_b, gate_b, w_branch, w_out,
           norm_ffn_g, w_up, ffn_dw_w, ffn_dw_b, w_down):
    return _forward(x, norm_mix_g, w_in, fox_forget_b, fox_q_norm_g, fox_k_norm_g, hgrn_lb_logits,
                    hgrn_out_norm_g, conv_dw_w, conv_dw_b, conv_norm_g, conv_norm_b, gate_b, w_branch,
                    w_out, norm_ffn_g, w_up, ffn_dw_w, ffn_dw_b, w_down)
```

```python
import functools

import numpy as np
import jax
import jax.numpy as jnp
from jax import lax
from jax.experimental import pallas as pl
from jax.experimental.pallas import tpu as pltpu

D_MODEL = 1024
CHUNK = 64
FOX_HEADS = 6
FOX_HEAD_DIM = 64
FOX_WIDTH = FOX_HEADS * FOX_HEAD_DIM
HGRN_HEADS = 4
HGRN_KEY_DIM = 128
HGRN_VAL_DIM = 96
HGRN_KEY_WIDTH = HGRN_HEADS * HGRN_KEY_DIM
HGRN_VAL_WIDTH = HGRN_HEADS * HGRN_VAL_DIM
CONV_CH = 256
CONV_WIDTH = 31
N_BRANCH = 3
D_FF = 2816
FFN_CONV_WIDTH = 3
EPS = 1e-6

LANE = 128
SUBLANE = 8
VMEM_LIMIT_BYTES = 56 * 1024 * 1024

HGRN_VAL_PAD = LANE
HGRN_PAD_WIDTH = HGRN_HEADS * HGRN_VAL_PAD
FOX_AUG = LANE
FOX_VT_ROWS = 80
FOX_PAIRS = FOX_HEADS // 2
NEG_BIG = -1e30
LOG2_E = 1.4426950408889634

QKV_W = 3 * FOX_WIDTH
FF_W = LANE
HQIG_W = HGRN_KEY_WIDTH + 2 * HGRN_PAD_WIDTH
HF_W = HGRN_KEY_WIDTH
CAB_W = 2 * CONV_CH
GL_W = N_BRANCH * D_MODEL
IN_PROJ_W = QKV_W + FF_W + HQIG_W + HF_W + CAB_W + GL_W

f32 = jnp.float32
bf16 = jnp.bfloat16


def _cparams(semantics):
    return pltpu.CompilerParams(dimension_semantics=semantics, vmem_limit_bytes=VMEM_LIMIT_BYTES)


def _log_sigmoid(x):
    return jnp.minimum(x, 0.0) - jnp.log1p(jnp.exp(-jnp.abs(x)))


def _sigmoid(x):
    return 0.5 * jnp.tanh(0.5 * x) + 0.5


def _dot(a, b):
    return jnp.dot(a, b, preferred_element_type=f32)


def _dot_nt(a, b):
    return lax.dot_general(a, b, (((1,), (1,)), ((), ())), preferred_element_type=f32)


def _dot_tn(a, b):
    return lax.dot_general(a, b, (((0,), (0,)), ((), ())), preferred_element_type=f32)


def _split_bf16(x, n):
    parts = []
    r = x
    for _ in range(n):
        p = r.astype(bf16)
        parts.append(p)
        r = r - p.astype(f32)
    return parts


_IN_SIZES = (FOX_WIDTH, FOX_WIDTH, FOX_WIDTH, FOX_HEADS, HGRN_KEY_WIDTH, HGRN_KEY_WIDTH,
             HGRN_VAL_WIDTH, HGRN_VAL_WIDTH, CONV_CH, CONV_CH, N_BRANCH * D_MODEL)
_IN_OFFS = tuple(int(v) for v in np.cumsum((0,) + _IN_SIZES))
_W_PREP_ROWS = 128


def _w_in_prep_kernel(w_ref, o_ref):
    o_fq, _, _, o_ff, o_hq, o_hf, o_hi, o_hg, o_ca, _, o_gl, _ = _IN_OFFS
    rows = w_ref.shape[1]

    def src(a, width):
        return w_ref[0, :, a:a + width]

    def padded_heads(a):
        pieces = []
        for h in range(HGRN_HEADS):
            pieces.append(src(a + h * HGRN_VAL_DIM, HGRN_VAL_DIM))
            pieces.append(jnp.zeros((rows, HGRN_VAL_PAD - HGRN_VAL_DIM), f32))
        return jnp.concatenate(pieces, axis=1)

    segments = [
        src(o_fq, QKV_W),
        jnp.concatenate([src(o_ff, FOX_HEADS), jnp.zeros((rows, FF_W - FOX_HEADS), f32)], axis=1),
        src(o_hq, HGRN_KEY_WIDTH),
        padded_heads(o_hi),
        padded_heads(o_hg),
        src(o_hf, HGRN_KEY_WIDTH),
        src(o_ca, CAB_W),
        src(o_gl, GL_W),
    ]
    c0 = 0
    for seg in segments:
        o_ref[0, :, c0:c0 + seg.shape[1]] = seg.astype(bf16)
        c0 += seg.shape[1]


def _w_in_prep(w_in):
    depth, d, n = w_in.shape
    return pl.pallas_call(
        _w_in_prep_kernel,
        grid=(depth, d // _W_PREP_ROWS),
        in_specs=[pl.BlockSpec((1, _W_PREP_ROWS, n), lambda l, i: (l, i, 0))],
        out_specs=pl.BlockSpec((1, _W_PREP_ROWS, IN_PROJ_W), lambda l, i: (l, i, 0)),
        out_shape=jax.ShapeDtypeStruct((depth, d, IN_PROJ_W), bf16),
        compiler_params=_cparams(("parallel", "parallel")),
        name="w_in_prep",
    )(w_in)


def _in_proj_kernel(x_ref, g_ref, w_ref, qkv_ref, ff_ref, hqig_ref, hf_ref, cab_ref, gl_ref):
    x = x_ref[...]
    h = (x * lax.rsqrt(jnp.mean(x * x, axis=-1, keepdims=True) + EPS) * g_ref[...]).astype(bf16)

    c0 = 0
    z = _dot(h, w_ref[:, c0:c0 + QKV_W + FF_W])
    qkv_ref[...] = z[:, :QKV_W].astype(bf16)
    ff_ref[...] = z[:, QKV_W:]
    c0 += QKV_W + FF_W
    hqig_ref[...] = _dot(h, w_ref[:, c0:c0 + HQIG_W]).astype(bf16)
    c0 += HQIG_W
    hf_ref[...] = _dot(h, w_ref[:, c0:c0 + HF_W])
    c0 += HF_W
    cab_ref[...] = _dot(h, w_ref[:, c0:c0 + CAB_W]).astype(bf16)
    c0 += CAB_W
    half = GL_W // 2
    for a in (0, half):
        gl_ref[:, a:a + half] = _dot(h, w_ref[:, c0 + a:c0 + a + half]).astype(bf16)


def _in_proj(x2d, g, w_all, layer, tm):
    m = x2d.shape[0]
    widths = (QKV_W, FF_W, HQIG_W, HF_W, CAB_W, GL_W)
    dtypes = (bf16, f32, bf16, f32, bf16, bf16)
    return pl.pallas_call(
        _in_proj_kernel,
        grid=(m // tm,),
        in_specs=[
            pl.BlockSpec((tm, D_MODEL), lambda i: (i, 0)),
            pl.BlockSpec((1, D_MODEL), lambda i: (0, 0)),
            pl.BlockSpec((None, D_MODEL, IN_PROJ_W), lambda i: (layer, 0, 0), pipeline_mode=pl.Buffered(1)),
        ],
        out_specs=[pl.BlockSpec((tm, wd), lambda i: (i, 0)) for wd in widths],
        out_shape=[jax.ShapeDtypeStruct((m, wd), dt) for wd, dt in zip(widths, dtypes)],
        compiler_params=_cparams(("parallel",)),
        name="in_proj",
    )(x2d, g, w_all)


def _fox_prep_kernel(qkv_ref, ff_ref, fb_ref, gq_ref, gk_ref, tri_ref,
                     q_out, k_out, vt_out, carry_ref, *, ts):
    @pl.when(pl.program_id(1) == 0)
    def _():
        carry_ref[...] = jnp.zeros_like(carry_ref)

    lane = lax.broadcasted_iota(jnp.int32, (ts, LANE), 1)
    low = lane < FOX_HEAD_DIM

    lf = _log_sigmoid(ff_ref[0] + fb_ref[...])
    parts = jnp.concatenate(_split_bf16(lf, 3), axis=1)
    cs = _dot(tri_ref[...], parts)
    c = (cs[:, :LANE] + carry_ref[...]) + cs[:, LANE:2 * LANE] + cs[:, 2 * LANE:]
    carry_ref[...] = c[ts - 1:ts, :]

    def normed(col0, gain_ref, scale):
        blocks = []
        for j in range(FOX_PAIRS):
            blk = qkv_ref[0, :, col0 + j * LANE:col0 + (j + 1) * LANE].astype(f32)
            sq = blk * blk
            s_lo = jnp.sum(jnp.where(low, sq, 0.0), axis=1, keepdims=True)
            s_hi = jnp.sum(jnp.where(low, 0.0, sq), axis=1, keepdims=True)
            rs = jnp.where(low, lax.rsqrt(s_lo * (1.0 / FOX_HEAD_DIM) + EPS),
                           lax.rsqrt(s_hi * (1.0 / FOX_HEAD_DIM) + EPS))
            blocks.append(blk * rs * (gain_ref[:, j * LANE:(j + 1) * LANE] * scale))
        return blocks

    qn = normed(0, gq_ref, FOX_HEAD_DIM ** -0.5 * LOG2_E)
    kn = normed(FOX_WIDTH, gk_ref, 1.0)
    c2 = c * LOG2_E

    for h in range(FOX_HEADS):
        cb = jnp.broadcast_to(c2[:, h:h + 1], (ts, LANE))
        c3 = [p.astype(f32) for p in _split_bf16(cb, 3)]

        def head_part(blocks):
            blk = blocks[h // 2]
            if h % 2 == 1:
                blk = pltpu.roll(blk, FOX_HEAD_DIM, axis=1)
            return blk

        qa = jnp.where((lane >= FOX_HEAD_DIM + 3) & (lane < FOX_HEAD_DIM + 6), 1.0, 0.0)
        ka = jnp.where((lane >= FOX_HEAD_DIM) & (lane < FOX_HEAD_DIM + 3), 1.0, 0.0)
        for i in range(3):
            qa = jnp.where(lane == FOX_HEAD_DIM + i, c3[i], qa)
            ka = jnp.where(lane == FOX_HEAD_DIM + 3 + i, -c3[i], ka)
        q_out[0, h] = jnp.where(low, head_part(qn), qa).astype(bf16)
        k_out[0, h] = jnp.where(low, head_part(kn), ka).astype(bf16)

    vt = qkv_ref[0, :, 2 * FOX_WIDTH:3 * FOX_WIDTH].astype(f32).T
    tail_row = lax.broadcasted_iota(jnp.int32, (FOX_VT_ROWS - FOX_HEAD_DIM, ts), 0)
    tail = jnp.where(tail_row == 0, 1.0, 0.0).astype(bf16)
    for h in range(FOX_HEADS):
        vt_out[0, h, 0:FOX_HEAD_DIM, :] = vt[h * FOX_HEAD_DIM:(h + 1) * FOX_HEAD_DIM, :].astype(bf16)
        vt_out[0, h, FOX_HEAD_DIM:FOX_VT_ROWS, :] = tail


def _fox_prep(qkv, ff, fb, gq, gk, tri, ts):
    b, s, _ = qkv.shape
    kern = functools.partial(_fox_prep_kernel, ts=ts)
    return pl.pallas_call(
        kern,
        grid=(b, s // ts),
        in_specs=[
            pl.BlockSpec((1, ts, QKV_W), lambda i, j: (i, j, 0)),
            pl.BlockSpec((1, ts, FF_W), lambda i, j: (i, j, 0)),
            pl.BlockSpec((1, LANE), lambda i, j: (0, 0)),
            pl.BlockSpec((1, FOX_WIDTH), lambda i, j: (0, 0)),
            pl.BlockSpec((1, FOX_WIDTH), lambda i, j: (0, 0)),
            pl.BlockSpec((ts, ts), lambda i, j: (0, 0)),
        ],
        out_specs=[
            pl.BlockSpec((1, FOX_HEADS, ts, FOX_AUG), lambda i, j: (i, 0, j, 0)),
            pl.BlockSpec((1, FOX_HEADS, ts, FOX_AUG), lambda i, j: (i, 0, j, 0)),
            pl.BlockSpec((1, FOX_HEADS, FOX_VT_ROWS, ts), lambda i, j: (i, 0, 0, j)),
        ],
        out_shape=[
            jax.ShapeDtypeStruct((b, FOX_HEADS, s, FOX_AUG), bf16),
            jax.ShapeDtypeStruct((b, FOX_HEADS, s, FOX_AUG), bf16),
            jax.ShapeDtypeStruct((b, FOX_HEADS, FOX_VT_ROWS, s), bf16),
        ],
        scratch_shapes=[pltpu.VMEM((1, LANE), f32)],
        compiler_params=_cparams(("parallel", "arbitrary")),
        name="fox_prep",
    )(qkv, ff, fb, gq, gk, tri)


def _fox_attn_kernel(q_ref, k_ref, vt_ref, o_ref, m_ref, acc_ref, s_ref, *, tq, tk):
    qi = pl.program_id(2)
    nsub = tq // tk
    key_idx = lax.broadcasted_iota(jnp.int32, (tk, tk), 0)
    qry_idx = lax.broadcasted_iota(jnp.int32, (tk, tk), 1)
    causal = key_idx <= qry_idx
    m_ref[...] = jnp.full(m_ref.shape, NEG_BIG, f32)
    acc_ref[...] = jnp.zeros(acc_ref.shape, f32)

    def scores(hh, kb, lo):
        start = pl.multiple_of(kb * tk, tk)
        return _dot_nt(k_ref[0, hh, pl.ds(start, tk), :], q_ref[0, hh, lo:tq, :])

    def update(hh, kb, lo, st, diagonal):
        start = pl.multiple_of(kb * tk, tk)
        if diagonal:
            head = jnp.where(causal, st[:, :tk], NEG_BIG)
            st = head if lo + tk == tq else jnp.concatenate([head, st[:, tk:]], axis=1)
        m_old = m_ref[hh, :, lo:tq]
        m_new = jnp.maximum(m_old, jnp.max(st, axis=0, keepdims=True))
        p = jnp.exp2(st - m_new).astype(bf16)
        pv = _dot(vt_ref[0, hh, :, pl.ds(start, tk)], p)
        acc_ref[hh, :, lo:tq] = jnp.exp2(m_old - m_new) * acc_ref[hh, :, lo:tq] + pv
        m_ref[hh, :, lo:tq] = m_new

    n_main = qi * nsub
    for hh in range(2):
        s_ref[0, hh] = scores(hh, 0, 0)

    def body(it, carry):
        kb = it * 2
        for slot in range(2):
            for hh in range(2):
                s_ref[1 - slot, hh] = scores(hh, kb + slot + 1, 0)
                update(hh, kb + slot, 0, s_ref[slot, hh], False)
        return carry

    lax.fori_loop(0, n_main // 2, body, 0)
    for e in range(nsub):
        for hh in range(2):
            if e + 1 < nsub:
                s_ref[(e + 1) % 2, hh, :, (e + 1) * tk:tq] = scores(hh, n_main + e + 1, (e + 1) * tk)
            update(hh, n_main + e, e * tk, s_ref[e % 2, hh, :, e * tk:tq], True)

    outs = [acc_ref[hh, 0:FOX_HEAD_DIM, :] / acc_ref[hh, FOX_HEAD_DIM:FOX_HEAD_DIM + 1, :] for hh in range(2)]
    o_ref[0] = jnp.concatenate(outs, axis=0).T.astype(bf16)


def _fox_attn(qa, ka, vt, tq, tk):
    b, _, s, _ = qa.shape
    kern = functools.partial(_fox_attn_kernel, tq=tq, tk=tk)
    return pl.pallas_call(
        kern,
        grid=(b, FOX_PAIRS, s // tq),
        in_specs=[
            pl.BlockSpec((1, 2, tq, FOX_AUG), lambda i, p, j: (i, p, j, 0)),
            pl.BlockSpec((1, 2, s, FOX_AUG), lambda i, p, j: (i, p, 0, 0)),
            pl.BlockSpec((1, 2, FOX_VT_ROWS, s), lambda i, p, j: (i, p, 0, 0)),
        ],
        out_specs=pl.BlockSpec((1, tq, LANE), lambda i, p, j: (i, j, p)),
        out_shape=jax.ShapeDtypeStruct((b, s, FOX_WIDTH), bf16),
        scratch_shapes=[pltpu.VMEM((2, 1, tq), f32), pltpu.VMEM((2, FOX_VT_ROWS, tq), f32),
                        pltpu.VMEM((2, 2, tk, tq), f32)],
        compiler_params=_cparams(("parallel", "parallel", "arbitrary")),
        name="fox_attn",
    )(qa, ka, vt)


_HGRN_HALVES = (1, 2, 4, 8, 16, 32)
_HGRN_MXU_LEVELS = 3


def _hgrn_prefix_matrix():
    mats = []
    for hsz in _HGRN_HALVES[:_HGRN_MXU_LEVELS]:
        m = np.zeros((CHUNK, CHUNK), np.float32)
        for t in range(CHUNK):
            bd = (t // (2 * hsz)) * 2 * hsz + hsz - 1
            if t % (2 * hsz) >= hsz:
                m[t, bd + 1:t + 1] = 1.0
            else:
                m[t, t + 1:bd + 1] = 1.0
        mats.append(m)
    mats.append(np.tril(np.ones((CHUNK, CHUNK), np.float32)))
    return np.concatenate(mats, axis=0)


def _hgrn_kernel(hqig_ref, hf_ref, llb_ref, l1m_ref, gn_ref, cmat_ref, y_ref, st_ref, *, n_chunks):
    @pl.when(pl.program_id(1) == 0)
    def _():
        st_ref[...] = jnp.zeros_like(st_ref)

    t_i = lax.broadcasted_iota(jnp.int32, (CHUNK, CHUNK), 0)
    s_i = lax.broadcasted_iota(jnp.int32, (CHUNK, CHUNK), 1)
    masks = []
    for hsz in _HGRN_HALVES:
        same = (t_i // (2 * hsz)) == (s_i // (2 * hsz))
        masks.append(same & ((t_i % (2 * hsz)) >= hsz) & ((s_i % (2 * hsz)) < hsz))
    cmat = cmat_ref[...]
    kw, vw = HGRN_KEY_WIDTH, HGRN_PAD_WIDTH

    def chunk(c, carry):
        r0 = pl.multiple_of(c * CHUNK, CHUNK)
        rows = pl.ds(r0, CHUNK)
        for h in range(HGRN_HEADS):
            kcols = slice(h * HGRN_KEY_DIM, (h + 1) * HGRN_KEY_DIM)
            q = hqig_ref[0, rows, kcols].astype(f32)
            v = hqig_ref[0, rows, kw + h * HGRN_VAL_PAD:kw + (h + 1) * HGRN_VAL_PAD]
            g = hqig_ref[0, rows, kw + vw + h * HGRN_VAL_PAD:kw + vw + (h + 1) * HGRN_VAL_PAD].astype(f32)
            x = hf_ref[0, rows, kcols]

            ls = jnp.minimum(x, 0.0) - jnp.log(1.0 + jnp.exp(-jnp.abs(x)))
            a = llb_ref[:, kcols]
            bb = l1m_ref[:, kcols] + ls
            logf = jnp.maximum(a, bb) + jnp.log(1.0 + jnp.exp(-jnp.abs(a - bb)))
            kk = jnp.exp(l1m_ref[:, kcols] + ls - x)

            lh = jnp.concatenate(_split_bf16(logf, 2), axis=1)
            e = _dot(cmat, lh)
            e = e[:, :HGRN_KEY_DIM] + e[:, HGRN_KEY_DIM:]
            nl = _HGRN_MXU_LEVELS
            b = e[nl * CHUNK:(nl + 1) * CHUNK]
            decays = [jnp.exp(e[l * CHUNK:(l + 1) * CHUNK]) for l in range(nl)]
            for hsz in _HGRN_HALVES[nl:]:
                bc = jnp.concatenate(
                    [jnp.broadcast_to(b[blk + hsz - 1:blk + hsz, :], (2 * hsz, HGRN_KEY_DIM))
                     for blk in range(0, CHUNK, 2 * hsz)], axis=0)
                decays.append(jnp.exp(-jnp.abs(b - bc)))

            attn = jnp.zeros((CHUNK, CHUNK), f32)
            for mask, dec in zip(masks, decays):
                sc = _dot_nt((q * dec).astype(bf16), (kk * dec).astype(bf16))
                attn = attn + jnp.where(mask, sc, 0.0)
            diag = jnp.sum(q * kk, axis=1, keepdims=True)

            state_t = st_ref[h]
            o = (_dot(attn.astype(bf16), v) + diag * v.astype(f32)
                 + _dot_nt((q * jnp.exp(b)).astype(bf16), state_t.astype(bf16)))
            b_last = b[CHUNK - 1:CHUNK, :]
            kd = (kk * jnp.exp(b_last - b)).astype(bf16)
            st_ref[h] = jnp.exp(b_last) * state_t + _dot_tn(v, kd)

            ms = jnp.sum(o * o, axis=1, keepdims=True) * (1.0 / HGRN_VAL_DIM)
            vcols = slice(h * HGRN_VAL_PAD, (h + 1) * HGRN_VAL_PAD)
            y = o * lax.rsqrt(ms + EPS) * gn_ref[:, vcols] * (g * _sigmoid(g))
            y_ref[0, rows, vcols] = y.astype(bf16)
        return carry

    lax.fori_loop(0, n_chunks, chunk, 0, unroll=4)


def _hgrn(hqig, hf, llb, l1m, gn, cmat, tb):
    b, s, _ = hqig.shape
    kern = functools.partial(_hgrn_kernel, n_chunks=tb // CHUNK)
    return pl.pallas_call(
        kern,
        grid=(b, s // tb),
        in_specs=[
            pl.BlockSpec((1, tb, HQIG_W), lambda i, j: (i, j, 0)),
            pl.BlockSpec((1, tb, HF_W), lambda i, j: (i, j, 0)),
            pl.BlockSpec((1, HGRN_KEY_WIDTH), lambda i, j: (0, 0)),
            pl.BlockSpec((1, HGRN_KEY_WIDTH), lambda i, j: (0, 0)),
            pl.BlockSpec((1, HGRN_PAD_WIDTH), lambda i, j: (0, 0)),
            pl.BlockSpec(cmat.shape, lambda i, j: (0, 0)),
        ],
        out_specs=pl.BlockSpec((1, tb, HGRN_PAD_WIDTH), lambda i, j: (i, j, 0)),
        out_shape=jax.ShapeDtypeStruct((b, s, HGRN_PAD_WIDTH), bf16),
        scratch_shapes=[pltpu.VMEM((HGRN_HEADS, HGRN_VAL_PAD, HGRN_KEY_DIM), f32)],
        compiler_params=_cparams(("parallel", "arbitrary")),
        name="hgrn",
    )(hqig, hf, llb, l1m, gn, cmat)


_CONV_HIST = 32
_CONV_ROWS = 64


def _conv_kernel(cab_ref, w_ref, b_ref, g_ref, beta_ref, y_ref, ubuf_ref, shift_ref, *, ts):
    @pl.when(pl.program_id(1) == 0)
    def _():
        ubuf_ref[0:_CONV_HIST, :] = jnp.zeros((_CONV_HIST, CONV_CH), f32)

    ca = cab_ref[0, :, 0:CONV_CH].astype(f32)
    cb = cab_ref[0, :, CONV_CH:2 * CONV_CH].astype(f32)
    ubuf_ref[_CONV_HIST:_CONV_HIST + ts, :] = ca * _sigmoid(cb)

    span = ts + _CONV_HIST - SUBLANE
    for r in range(1, SUBLANE):
        shift_ref[r - 1] = ubuf_ref[r:r + span, :]

    first = _CONV_HIST - (CONV_WIDTH - 1)
    for r0 in range(0, ts, _CONV_ROWS):
        acc = jnp.broadcast_to(b_ref[...], (_CONV_ROWS, CONV_CH))
        for j in range(CONV_WIDTH):
            a, r = divmod(first + j, SUBLANE)
            src = ubuf_ref if r == 0 else shift_ref.at[r - 1]
            acc = acc + w_ref[j:j + 1, :] * src[r0 + a * SUBLANE:r0 + a * SUBLANE + _CONV_ROWS, :]
        mu = jnp.mean(acc, axis=-1, keepdims=True)
        xc = acc - mu
        yn = xc * lax.rsqrt(jnp.mean(xc * xc, axis=-1, keepdims=True) + EPS)
        yn = yn * g_ref[...] + beta_ref[...]
        y_ref[0, r0:r0 + _CONV_ROWS, :] = (yn * _sigmoid(yn)).astype(bf16)

    ubuf_ref[0:_CONV_HIST, :] = ubuf_ref[ts:ts + _CONV_HIST, :]


def _conv(cab, w, bias, g, beta, ts):
    b, s, _ = cab.shape
    kern = functools.partial(_conv_kernel, ts=ts)
    vec = pl.BlockSpec((1, CONV_CH), lambda i, j: (0, 0))
    return pl.pallas_call(
        kern,
        grid=(b, s // ts),
        in_specs=[
            pl.BlockSpec((1, ts, CAB_W), lambda i, j: (i, j, 0)),
            pl.BlockSpec((CONV_WIDTH, CONV_CH), lambda i, j: (0, 0)),
            vec, vec, vec,
        ],
        out_specs=pl.BlockSpec((1, ts, CONV_CH), lambda i, j: (i, j, 0)),
        out_shape=jax.ShapeDtypeStruct((b, s, CONV_CH), bf16),
        scratch_shapes=[pltpu.VMEM((ts + _CONV_HIST, CONV_CH), f32),
                        pltpu.VMEM((SUBLANE - 1, ts + _CONV_HIST - SUBLANE, CONV_CH), f32)],
        compiler_params=_cparams(("parallel", "arbitrary")),
        name="conv",
    )(cab, w, bias, g, beta)


def _merge_kernel(ya_ref, yb_ref, yc_ref, gl_ref, gb_ref, x_ref, wa_ref, wb_ref, wc_ref, wo_ref,
                  gn_ref, xo_ref, h_ref):
    merged = None
    for i, (y_ref, w_ref) in enumerate(((ya_ref, wa_ref), (yb_ref, wb_ref), (yc_ref, wc_ref))):
        cols = slice(i * D_MODEL, (i + 1) * D_MODEL)
        gate = _sigmoid(gl_ref[:, cols].astype(f32) + gb_ref[:, cols])
        term = gate * _dot(y_ref[...], w_ref[...])
        merged = term if merged is None else merged + term
    x = x_ref[...] + _dot(merged.astype(bf16), wo_ref[...])
    xo_ref[...] = x
    h_ref[...] = (x * lax.rsqrt(jnp.mean(x * x, axis=-1, keepdims=True) + EPS) * gn_ref[...]).astype(bf16)


def _merge(ya, yb, yc, gl, gate_b, x2d, wa, wb, wc, wo, gn, tm):
    m = x2d.shape[0]
    row = lambda wd: pl.BlockSpec((tm, wd), lambda i: (i, 0))
    full = lambda a: pl.BlockSpec(a.shape, lambda i: (0, 0))
    return pl.pallas_call(
        _merge_kernel,
        grid=(m // tm,),
        in_specs=[row(FOX_WIDTH), row(HGRN_PAD_WIDTH), row(CONV_CH), row(GL_W), full(gate_b),
                  row(D_MODEL), full(wa), full(wb), full(wc), full(wo), full(gn)],
        out_specs=[row(D_MODEL), row(D_MODEL)],
        out_shape=[jax.ShapeDtypeStruct((m, D_MODEL), f32), jax.ShapeDtypeStruct((m, D_MODEL), bf16)],
        compiler_params=_cparams(("parallel",)),
        name="merge",
    )(ya, yb, yc, gl, gate_b, x2d, wa, wb, wc, wo, gn)


_FFN_HALO = 2 * SUBLANE
_FFN_COLS = 256


def _ffn_kernel(h_ref, hprev_ref, x_ref, wup_ref, cw_ref, cb_ref, wd_ref, o_ref, gbuf_ref, act_ref,
                *, tm, blocks_per_seq):
    i = pl.program_id(0)
    seq_start = (i % blocks_per_seq) == 0
    h_ext = jnp.concatenate([hprev_ref[...], h_ref[...]], axis=0)
    row = lax.broadcasted_iota(jnp.int32, (tm + _FFN_HALO, _FFN_COLS), 0)
    keep = jnp.logical_or(row >= _FFN_HALO, jnp.logical_not(seq_start))
    for n, c0 in enumerate(range(0, D_FF, _FFN_COLS)):
        cols = slice(c0, c0 + _FFN_COLS)
        gbuf = gbuf_ref.at[n % 2]
        g_ext = _dot(h_ext, wup_ref[:, cols])
        gbuf[...] = jnp.where(keep, g_ext, 0.0)
        conv = cb_ref[:, cols] + cw_ref[FFN_CONV_WIDTH - 1:FFN_CONV_WIDTH, cols] * g_ext[_FFN_HALO:]
        for k in range(FFN_CONV_WIDTH - 1):
            off = _FFN_HALO - (FFN_CONV_WIDTH - 1) + k
            conv = conv + cw_ref[k:k + 1, cols] * gbuf[off:off + tm, :]
        val = _dot(h_ref[...], wup_ref[:, D_FF + c0:D_FF + c0 + _FFN_COLS])
        act_ref[:, cols] = (conv * _sigmoid(conv) * val).astype(bf16)
    o_ref[...] = x_ref[...] + _dot(act_ref[...], wd_ref[...])


def _ffn(h2, x2d, wup, cw, cb, wd, tm, seq):
    m = x2d.shape[0]
    kern = functools.partial(_ffn_kernel, tm=tm, blocks_per_seq=seq // tm)
    halo_blocks = tm // _FFN_HALO
    resident = lambda a: pl.BlockSpec(a.shape, lambda i: (0, 0), pipeline_mode=pl.Buffered(1))
    return pl.pallas_call(
        kern,
        grid=(m // tm,),
        in_specs=[
            pl.BlockSpec((tm, D_MODEL), lambda i: (i, 0)),
            pl.BlockSpec((_FFN_HALO, D_MODEL), lambda i: (jnp.maximum(i * halo_blocks - 1, 0), 0)),
            pl.BlockSpec((tm, D_MODEL), lambda i: (i, 0)),
            resident(wup), resident(cw), resident(cb), resident(wd),
        ],
        out_specs=pl.BlockSpec((tm, D_MODEL), lambda i: (i, 0)),
        out_shape=jax.ShapeDtypeStruct((m, D_MODEL), f32),
        scratch_shapes=[pltpu.VMEM((2, tm + _FFN_HALO, _FFN_COLS), f32), pltpu.VMEM((tm, D_FF), bf16)],
        compiler_params=_cparams(("parallel",)),
        name="ffn",
    )(h2, h2, x2d, wup, cw, cb, wd)


def _pad_heads(w, axis):
    shape = list(w.shape)
    shape[axis:axis + 1] = [HGRN_HEADS, HGRN_VAL_DIM]
    w = w.reshape(shape)
    pad = [(0, 0)] * len(shape)
    pad[axis + 1] = (0, HGRN_VAL_PAD - HGRN_VAL_DIM)
    w = jnp.pad(w, pad)
    shape[axis:axis + 2] = [HGRN_PAD_WIDTH]
    return w.reshape(shape)


def _block(total, want):
    return min(total, want)


@jax.jit
def _forward(x, norm_mix_g, w_in, fox_forget_b, fox_q_norm_g, fox_k_norm_g, hgrn_lb_logits,
             hgrn_out_norm_g, conv_dw_w, conv_dw_b, conv_norm_g, conv_norm_b, gate_b, w_branch,
             w_out, norm_ffn_g, w_up, ffn_dw_w, ffn_dw_b, w_down):
    b, s, d = x.shape
    depth = w_in.shape[0]
    m = b * s

    tm_in = _block(m, 512)
    ts_prep = _block(s, 512)
    tq = _block(s, 1024)
    tk = 256
    tb_hgrn = _block(s, 512)
    ts_conv = _block(s, 256)
    tm_merge = _block(m, 512)
    tm_ffn = _block(s, 512)

    lb_all = jnp.cumsum(jax.nn.softmax(hgrn_lb_logits.astype(f32), axis=0), axis=0)
    lb_all = jnp.maximum(lb_all - lb_all[0:1], 0.0)
    log_lb = jnp.log(lb_all)
    log_1m_lb = jnp.log1p(-lb_all)

    tri = jnp.asarray(np.tril(np.ones((ts_prep, ts_prep), np.float32)), bf16)
    cmat = jnp.asarray(_hgrn_prefix_matrix(), bf16)
    o_a = FOX_WIDTH
    o_b = FOX_WIDTH + HGRN_VAL_WIDTH

    w_in_cat = _w_in_prep(w_in)
    w_up_bf = w_up.astype(bf16)
    w_down_bf = w_down.astype(bf16)

    x2d = x.reshape(m, d)
    for l in range(depth):
        qkv, ff, hqig, hf, cab, gl = _in_proj(x2d, norm_mix_g[l][None, :], w_in_cat, l, tm_in)

        fb = jnp.pad(fox_forget_b[l], (0, LANE - FOX_HEADS))[None, :]
        gq = jnp.tile(fox_q_norm_g[l], FOX_HEADS)[None, :]
        gk = jnp.tile(fox_k_norm_g[l], FOX_HEADS)[None, :]
        qa, ka, vt = _fox_prep(qkv.reshape(b, s, QKV_W), ff.reshape(b, s, FF_W), fb, gq, gk, tri, ts_prep)
        ya = _fox_attn(qa, ka, vt, tq, tk)

        gn = jnp.tile(jnp.pad(hgrn_out_norm_g[l], (0, HGRN_VAL_PAD - HGRN_VAL_DIM)), HGRN_HEADS)[None, :]
        yb = _hgrn(hqig.reshape(b, s, HQIG_W), hf.reshape(b, s, HF_W), log_lb[l][None, :],
                   log_1m_lb[l][None, :], gn, cmat, tb_hgrn)

        yc = _conv(cab.reshape(b, s, CAB_W), conv_dw_w[l], conv_dw_b[l][None, :],
                   conv_norm_g[l][None, :], conv_norm_b[l][None, :], ts_conv)

        wb = w_branch[l]
        x2d, h2 = _merge(ya.reshape(m, FOX_WIDTH), yb.reshape(m, HGRN_PAD_WIDTH), yc.reshape(m, CONV_CH),
                         gl, gate_b[l][None, :], x2d,
                         wb[:o_a].astype(bf16), _pad_heads(wb[o_a:o_b], 0).astype(bf16),
                         wb[o_b:].astype(bf16), w_out[l].astype(bf16), norm_ffn_g[l][None, :], tm_merge)

        x2d = _ffn(h2, x2d, w_up_bf[l], ffn_dw_w[l], ffn_dw_b[l][None, :], w_down_bf[l], tm_ffn, s)
    return x2d.reshape(b, s, d)


def kernel(x, norm_mix_g, w_in, fox_forget_b, fox_q_norm_g, fox_k_norm_g, hgrn_lb_logits,
           hgrn_out_norm_g, conv_dw_w, conv_dw_b, conv_norm_g, conv_norm_b, gate_b, w_branch, w_out,
           norm_ffn_g, w_up, ffn_dw_w, ffn_dw_b, w_down):
    return _forward(x, norm_mix_g, w_in, fox_forget_b, fox_q_norm_g, fox_k_norm_g, hgrn_lb_logits,
                    hgrn_out_norm_g, conv_dw_w, conv_dw_b, conv_norm_g, conv_norm_b, gate_b, w_branch,
                    w_out, norm_ffn_g, w_up, ffn_dw_w, ffn_dw_b, w_down)
```

```python
import functools

import numpy as np
import jax
import jax.numpy as jnp
from jax import lax
from jax.experimental import pallas as pl
from jax.experimental.pallas import tpu as pltpu

D_MODEL = 1024
CHUNK = 64
FOX_HEADS = 6
FOX_HEAD_DIM = 64
FOX_WIDTH = FOX_HEADS * FOX_HEAD_DIM
HGRN_HEADS = 4
HGRN_KEY_DIM = 128
HGRN_VAL_DIM = 96
HGRN_KEY_WIDTH = HGRN_HEADS * HGRN_KEY_DIM
HGRN_VAL_WIDTH = HGRN_HEADS * HGRN_VAL_DIM
CONV_CH = 256
CONV_WIDTH = 31
N_BRANCH = 3
D_FF = 2816
FFN_CONV_WIDTH = 3
EPS = 1e-6

LANE = 128
SUBLANE = 8
VMEM_LIMIT_BYTES = 56 * 1024 * 1024

HGRN_VAL_PAD = LANE
HGRN_PAD_WIDTH = HGRN_HEADS * HGRN_VAL_PAD
FOX_AUG = LANE
FOX_VT_ROWS = 80
FOX_PAIRS = FOX_HEADS // 2
NEG_BIG = -1e30
LOG2_E = 1.4426950408889634

QKV_W = 3 * FOX_WIDTH
FF_W = LANE
HQIG_W = HGRN_KEY_WIDTH + 2 * HGRN_PAD_WIDTH
HF_W = HGRN_KEY_WIDTH
CAB_W = 2 * CONV_CH
GL_W = N_BRANCH * D_MODEL
IN_PROJ_W = QKV_W + FF_W + HQIG_W + HF_W + CAB_W + GL_W

f32 = jnp.float32
bf16 = jnp.bfloat16


def _cparams(semantics):
    return pltpu.CompilerParams(dimension_semantics=semantics, vmem_limit_bytes=VMEM_LIMIT_BYTES)


def _log_sigmoid(x):
    return jnp.minimum(x, 0.0) - jnp.log1p(jnp.exp(-jnp.abs(x)))


def _sigmoid(x):
    return 0.5 * jnp.tanh(0.5 * x) + 0.5


def _silu(x):
    u = 0.5 * x
    return u * jnp.tanh(u) + u


def _dot(a, b):
    return jnp.dot(a, b, preferred_element_type=f32)


def _dot_nt(a, b):
    return lax.dot_general(a, b, (((1,), (1,)), ((), ())), preferred_element_type=f32)


def _dot_tn(a, b):
    return lax.dot_general(a, b, (((0,), (0,)), ((), ())), preferred_element_type=f32)


def _split_bf16(x, n):
    parts = []
    r = x
    for _ in range(n):
        p = r.astype(bf16)
        parts.append(p)
        r = r - p.astype(f32)
    return parts


_IN_SIZES = (FOX_WIDTH, FOX_WIDTH, FOX_WIDTH, FOX_HEADS, HGRN_KEY_WIDTH, HGRN_KEY_WIDTH,
             HGRN_VAL_WIDTH, HGRN_VAL_WIDTH, CONV_CH, CONV_CH, N_BRANCH * D_MODEL)
_IN_OFFS = tuple(int(v) for v in np.cumsum((0,) + _IN_SIZES))
_W_PREP_ROWS = 128


def _w_in_prep_kernel(w_ref, o_ref):
    o_fq, _, _, o_ff, o_hq, o_hf, o_hi, o_hg, o_ca, _, o_gl, _ = _IN_OFFS
    rows = w_ref.shape[1]

    def src(a, width):
        return w_ref[0, :, a:a + width]

    def padded_heads(a):
        pieces = []
        for h in range(HGRN_HEADS):
            pieces.append(src(a + h * HGRN_VAL_DIM, HGRN_VAL_DIM))
            pieces.append(jnp.zeros((rows, HGRN_VAL_PAD - HGRN_VAL_DIM), f32))
        return jnp.concatenate(pieces, axis=1)

    segments = [
        src(o_fq, QKV_W),
        jnp.concatenate([src(o_ff, FOX_HEADS), jnp.zeros((rows, FF_W - FOX_HEADS), f32)], axis=1),
        src(o_hq, HGRN_KEY_WIDTH),
        padded_heads(o_hi),
        padded_heads(o_hg),
        src(o_hf, HGRN_KEY_WIDTH),
        src(o_ca, CAB_W),
        src(o_gl, GL_W) * 0.5,
    ]
    c0 = 0
    for seg in segments:
        o_ref[0, :, c0:c0 + seg.shape[1]] = seg.astype(bf16)
        c0 += seg.shape[1]


def _w_in_prep(w_in):
    depth, d, n = w_in.shape
    return pl.pallas_call(
        _w_in_prep_kernel,
        grid=(depth, d // _W_PREP_ROWS),
        in_specs=[pl.BlockSpec((1, _W_PREP_ROWS, n), lambda l, i: (l, i, 0))],
        out_specs=pl.BlockSpec((1, _W_PREP_ROWS, IN_PROJ_W), lambda l, i: (l, i, 0)),
        out_shape=jax.ShapeDtypeStruct((depth, d, IN_PROJ_W), bf16),
        compiler_params=_cparams(("parallel", "parallel")),
        name="w_in_prep",
    )(w_in)


def _in_proj_kernel(x_ref, g_ref, w_ref, qkv_ref, ff_ref, hqig_ref, hf_ref, cab_ref, gl_ref):
    x = x_ref[...]
    h = (x * lax.rsqrt(jnp.mean(x * x, axis=-1, keepdims=True) + EPS) * g_ref[...]).astype(bf16)

    c0 = 0
    z = _dot(h, w_ref[:, c0:c0 + QKV_W + FF_W])
    qkv_ref[...] = z[:, :QKV_W].astype(bf16)
    ff_ref[...] = z[:, QKV_W:]
    c0 += QKV_W + FF_W
    hqig_ref[...] = _dot(h, w_ref[:, c0:c0 + HQIG_W]).astype(bf16)
    c0 += HQIG_W
    hf_ref[...] = _dot(h, w_ref[:, c0:c0 + HF_W])
    c0 += HF_W
    cab_ref[...] = _dot(h, w_ref[:, c0:c0 + CAB_W]).astype(bf16)
    c0 += CAB_W
    half = GL_W // 2
    for a in (0, half):
        gl_ref[:, a:a + half] = _dot(h, w_ref[:, c0 + a:c0 + a + half]).astype(bf16)


def _in_proj(x2d, g, w_all, layer, tm):
    m = x2d.shape[0]
    widths = (QKV_W, FF_W, HQIG_W, HF_W, CAB_W, GL_W)
    dtypes = (bf16, f32, bf16, f32, bf16, bf16)
    return pl.pallas_call(
        _in_proj_kernel,
        grid=(m // tm,),
        in_specs=[
            pl.BlockSpec((tm, D_MODEL), lambda i: (i, 0)),
            pl.BlockSpec((1, D_MODEL), lambda i: (0, 0)),
            pl.BlockSpec((None, D_MODEL, IN_PROJ_W), lambda i: (layer, 0, 0), pipeline_mode=pl.Buffered(1)),
        ],
        out_specs=[pl.BlockSpec((tm, wd), lambda i: (i, 0)) for wd in widths],
        out_shape=[jax.ShapeDtypeStruct((m, wd), dt) for wd, dt in zip(widths, dtypes)],
        compiler_params=_cparams(("parallel",)),
        name="in_proj",
    )(x2d, g, w_all)


def _fox_prep_kernel(qkv_ref, ff_ref, fb_ref, gq_ref, gk_ref, tri_ref,
                     q_out, k_out, vt_out, carry_ref, *, ts):
    @pl.when(pl.program_id(1) == 0)
    def _():
        carry_ref[...] = jnp.zeros_like(carry_ref)

    lane = lax.broadcasted_iota(jnp.int32, (ts, LANE), 1)
    low = lane < FOX_HEAD_DIM

    lf = _log_sigmoid(ff_ref[0] + fb_ref[...])
    parts = jnp.concatenate(_split_bf16(lf, 3), axis=1)
    cs = _dot(tri_ref[...], parts)
    c = (cs[:, :LANE] + carry_ref[...]) + cs[:, LANE:2 * LANE] + cs[:, 2 * LANE:]
    carry_ref[...] = c[ts - 1:ts, :]

    def normed(col0, gain_ref, scale):
        blocks = []
        for j in range(FOX_PAIRS):
            blk = qkv_ref[0, :, col0 + j * LANE:col0 + (j + 1) * LANE].astype(f32)
            sq = blk * blk
            s_lo = jnp.sum(jnp.where(low, sq, 0.0), axis=1, keepdims=True)
            s_hi = jnp.sum(jnp.where(low, 0.0, sq), axis=1, keepdims=True)
            rs = jnp.where(low, lax.rsqrt(s_lo * (1.0 / FOX_HEAD_DIM) + EPS),
                           lax.rsqrt(s_hi * (1.0 / FOX_HEAD_DIM) + EPS))
            blocks.append(blk * rs * (gain_ref[:, j * LANE:(j + 1) * LANE] * scale))
        return blocks

    qn = normed(0, gq_ref, FOX_HEAD_DIM ** -0.5 * LOG2_E)
    kn = normed(FOX_WIDTH, gk_ref, 1.0)
    c2 = c * LOG2_E

    for h in range(FOX_HEADS):
        cb = jnp.broadcast_to(c2[:, h:h + 1], (ts, LANE))
        c3 = [p.astype(f32) for p in _split_bf16(cb, 3)]

        def head_part(blocks):
            blk = blocks[h // 2]
            if h % 2 == 1:
                blk = pltpu.roll(blk, FOX_HEAD_DIM, axis=1)
            return blk

        qa = jnp.where((lane >= FOX_HEAD_DIM + 3) & (lane < FOX_HEAD_DIM + 6), 1.0, 0.0)
        ka = jnp.where((lane >= FOX_HEAD_DIM) & (lane < FOX_HEAD_DIM + 3), 1.0, 0.0)
        for i in range(3):
            qa = jnp.where(lane == FOX_HEAD_DIM + i, c3[i], qa)
            ka = jnp.where(lane == FOX_HEAD_DIM + 3 + i, -c3[i], ka)
        q_out[0, h] = jnp.where(low, head_part(qn), qa).astype(bf16)
        k_out[0, h] = jnp.where(low, head_part(kn), ka).astype(bf16)

    vt = qkv_ref[0, :, 2 * FOX_WIDTH:3 * FOX_WIDTH].astype(f32).T
    tail_row = lax.broadcasted_iota(jnp.int32, (FOX_VT_ROWS - FOX_HEAD_DIM, ts), 0)
    tail = jnp.where(tail_row == 0, 1.0, 0.0).astype(bf16)
    for h in range(FOX_HEADS):
        vt_out[0, h, 0:FOX_HEAD_DIM, :] = vt[h * FOX_HEAD_DIM:(h + 1) * FOX_HEAD_DIM, :].astype(bf16)
        vt_out[0, h, FOX_HEAD_DIM:FOX_VT_ROWS, :] = tail


def _fox_prep(qkv, ff, fb, gq, gk, tri, ts):
    b, s, _ = qkv.shape
    kern = functools.partial(_fox_prep_kernel, ts=ts)
    return pl.pallas_call(
        kern,
        grid=(b, s // ts),
        in_specs=[
            pl.BlockSpec((1, ts, QKV_W), lambda i, j: (i, j, 0)),
            pl.BlockSpec((1, ts, FF_W), lambda i, j: (i, j, 0)),
            pl.BlockSpec((1, LANE), lambda i, j: (0, 0)),
            pl.BlockSpec((1, FOX_WIDTH), lambda i, j: (0, 0)),
            pl.BlockSpec((1, FOX_WIDTH), lambda i, j: (0, 0)),
            pl.BlockSpec((ts, ts), lambda i, j: (0, 0)),
        ],
        out_specs=[
            pl.BlockSpec((1, FOX_HEADS, ts, FOX_AUG), lambda i, j: (i, 0, j, 0)),
            pl.BlockSpec((1, FOX_HEADS, ts, FOX_AUG), lambda i, j: (i, 0, j, 0)),
            pl.BlockSpec((1, FOX_HEADS, FOX_VT_ROWS, ts), lambda i, j: (i, 0, 0, j)),
        ],
        out_shape=[
            jax.ShapeDtypeStruct((b, FOX_HEADS, s, FOX_AUG), bf16),
            jax.ShapeDtypeStruct((b, FOX_HEADS, s, FOX_AUG), bf16),
            jax.ShapeDtypeStruct((b, FOX_HEADS, FOX_VT_ROWS, s), bf16),
        ],
        scratch_shapes=[pltpu.VMEM((1, LANE), f32)],
        compiler_params=_cparams(("parallel", "arbitrary")),
        name="fox_prep",
    )(qkv, ff, fb, gq, gk, tri)


def _fox_attn_kernel(q_ref, k_ref, vt_ref, o_ref, m_ref, acc_ref, s_ref, *, tq, tk):
    qi = pl.program_id(2)
    nsub = tq // tk
    key_idx = lax.broadcasted_iota(jnp.int32, (tk, tk), 0)
    qry_idx = lax.broadcasted_iota(jnp.int32, (tk, tk), 1)
    causal = key_idx <= qry_idx
    m_ref[...] = jnp.full(m_ref.shape, NEG_BIG, f32)
    acc_ref[...] = jnp.zeros(acc_ref.shape, f32)

    def scores(hh, kb, lo):
        start = pl.multiple_of(kb * tk, tk)
        return _dot_nt(k_ref[0, hh, pl.ds(start, tk), :], q_ref[0, hh, lo:tq, :])

    def update(hh, kb, lo, st, diagonal):
        start = pl.multiple_of(kb * tk, tk)
        if diagonal:
            head = jnp.where(causal, st[:, :tk], NEG_BIG)
            st = head if lo + tk == tq else jnp.concatenate([head, st[:, tk:]], axis=1)
        m_old = m_ref[hh, :, lo:tq]
        m_new = jnp.maximum(m_old, jnp.max(st, axis=0, keepdims=True))
        p = jnp.exp2(st - m_new).astype(bf16)
        pv = _dot(vt_ref[0, hh, :, pl.ds(start, tk)], p)
        acc_ref[hh, :, lo:tq] = jnp.exp2(m_old - m_new) * acc_ref[hh, :, lo:tq] + pv
        m_ref[hh, :, lo:tq] = m_new

    n_main = qi * nsub
    for hh in range(2):
        s_ref[0, hh] = scores(hh, 0, 0)

    def body(it, carry):
        kb = it * 2
        for slot in range(2):
            for hh in range(2):
                s_ref[1 - slot, hh] = scores(hh, kb + slot + 1, 0)
                update(hh, kb + slot, 0, s_ref[slot, hh], False)
        return carry

    lax.fori_loop(0, n_main // 2, body, 0)
    for e in range(nsub):
        for hh in range(2):
            if e + 1 < nsub:
                s_ref[(e + 1) % 2, hh, :, (e + 1) * tk:tq] = scores(hh, n_main + e + 1, (e + 1) * tk)
            update(hh, n_main + e, e * tk, s_ref[e % 2, hh, :, e * tk:tq], True)

    outs = [acc_ref[hh, 0:FOX_HEAD_DIM, :] / acc_ref[hh, FOX_HEAD_DIM:FOX_HEAD_DIM + 1, :] for hh in range(2)]
    o_ref[0] = jnp.concatenate(outs, axis=0).T.astype(bf16)


def _fox_attn(qa, ka, vt, tq, tk):
    b, _, s, _ = qa.shape
    kern = functools.partial(_fox_attn_kernel, tq=tq, tk=tk)
    return pl.pallas_call(
        kern,
        grid=(b, FOX_PAIRS, s // tq),
        in_specs=[
            pl.BlockSpec((1, 2, tq, FOX_AUG), lambda i, p, j: (i, p, j, 0)),
            pl.BlockSpec((1, 2, s, FOX_AUG), lambda i, p, j: (i, p, 0, 0)),
            pl.BlockSpec((1, 2, FOX_VT_ROWS, s), lambda i, p, j: (i, p, 0, 0)),
        ],
        out_specs=pl.BlockSpec((1, tq, LANE), lambda i, p, j: (i, j, p)),
        out_shape=jax.ShapeDtypeStruct((b, s, FOX_WIDTH), bf16),
        scratch_shapes=[pltpu.VMEM((2, 1, tq), f32), pltpu.VMEM((2, FOX_VT_ROWS, tq), f32),
                        pltpu.VMEM((2, 2, tk, tq), f32)],
        compiler_params=_cparams(("parallel", "parallel", "arbitrary")),
        name="fox_attn",
    )(qa, ka, vt)


_HGRN_HALVES = (1, 2, 4, 8, 16, 32)
_HGRN_MXU_LEVELS = 3


def _hgrn_prefix_matrix():
    mats = []
    for hsz in _HGRN_HALVES[:_HGRN_MXU_LEVELS]:
        m = np.zeros((CHUNK, CHUNK), np.float32)
        for t in range(CHUNK):
            bd = (t // (2 * hsz)) * 2 * hsz + hsz - 1
            if t % (2 * hsz) >= hsz:
                m[t, bd + 1:t + 1] = 1.0
            else:
                m[t, t + 1:bd + 1] = 1.0
        mats.append(m)
    mats.append(np.tril(np.ones((CHUNK, CHUNK), np.float32)))
    return np.concatenate(mats, axis=0)


def _hgrn_kernel(hqig_ref, hf_ref, llb_ref, l1m_ref, gn_ref, cmat_ref, y_ref, st_ref, *, n_chunks):
    @pl.when(pl.program_id(1) == 0)
    def _():
        st_ref[...] = jnp.zeros_like(st_ref)

    t_i = lax.broadcasted_iota(jnp.int32, (CHUNK, CHUNK), 0)
    s_i = lax.broadcasted_iota(jnp.int32, (CHUNK, CHUNK), 1)
    masks = []
    for hsz in _HGRN_HALVES:
        same = (t_i // (2 * hsz)) == (s_i // (2 * hsz))
        masks.append(same & ((t_i % (2 * hsz)) >= hsz) & ((s_i % (2 * hsz)) < hsz))
    cmat = cmat_ref[...]
    kw, vw = HGRN_KEY_WIDTH, HGRN_PAD_WIDTH

    def chunk(c, carry):
        r0 = pl.multiple_of(c * CHUNK, CHUNK)
        rows = pl.ds(r0, CHUNK)
        for h in range(HGRN_HEADS):
            kcols = slice(h * HGRN_KEY_DIM, (h + 1) * HGRN_KEY_DIM)
            q = hqig_ref[0, rows, kcols].astype(f32)
            v = hqig_ref[0, rows, kw + h * HGRN_VAL_PAD:kw + (h + 1) * HGRN_VAL_PAD]
            g = hqig_ref[0, rows, kw + vw + h * HGRN_VAL_PAD:kw + vw + (h + 1) * HGRN_VAL_PAD].astype(f32)
            x = hf_ref[0, rows, kcols]

            ls = jnp.minimum(x, 0.0) - jnp.log(1.0 + jnp.exp(-jnp.abs(x)))
            a = llb_ref[:, kcols]
            bb = l1m_ref[:, kcols] + ls
            logf = jnp.maximum(a, bb) + jnp.log(1.0 + jnp.exp(-jnp.abs(a - bb)))
            kk = jnp.exp(l1m_ref[:, kcols] + ls - x)

            lh = jnp.concatenate(_split_bf16(logf, 2), axis=1)
            e = _dot(cmat, lh)
            e = e[:, :HGRN_KEY_DIM] + e[:, HGRN_KEY_DIM:]
            nl = _HGRN_MXU_LEVELS
            b = e[nl * CHUNK:(nl + 1) * CHUNK]
            decays = [jnp.exp(e[l * CHUNK:(l + 1) * CHUNK]) for l in range(nl)]
            for hsz in _HGRN_HALVES[nl:]:
                bc = jnp.concatenate(
                    [jnp.broadcast_to(b[blk + hsz - 1:blk + hsz, :], (2 * hsz, HGRN_KEY_DIM))
                     for blk in range(0, CHUNK, 2 * hsz)], axis=0)
                decays.append(jnp.exp(-jnp.abs(b - bc)))

            attn = jnp.zeros((CHUNK, CHUNK), f32)
            for mask, dec in zip(masks, decays):
                sc = _dot_nt((q * dec).astype(bf16), (kk * dec).astype(bf16))
                attn = attn + jnp.where(mask, sc, 0.0)
            diag = jnp.sum(q * kk, axis=1, keepdims=True)

            state_t = st_ref[h]
            o = (_dot(attn.astype(bf16), v) + diag * v.astype(f32)
                 + _dot_nt((q * jnp.exp(b)).astype(bf16), state_t.astype(bf16)))
            b_last = b[CHUNK - 1:CHUNK, :]
            kd = (kk * jnp.exp(b_last - b)).astype(bf16)
            st_ref[h] = jnp.exp(b_last) * state_t + _dot_tn(v, kd)

            ms = jnp.sum(o * o, axis=1, keepdims=True) * (1.0 / HGRN_VAL_DIM)
            vcols = slice(h * HGRN_VAL_PAD, (h + 1) * HGRN_VAL_PAD)
            y = o * lax.rsqrt(ms + EPS) * gn_ref[:, vcols] * _silu(g)
            y_ref[0, rows, vcols] = y.astype(bf16)
        return carry

    lax.fori_loop(0, n_chunks, chunk, 0, unroll=8)


def _hgrn(hqig, hf, llb, l1m, gn, cmat, tb):
    b, s, _ = hqig.shape
    kern = functools.partial(_hgrn_kernel, n_chunks=tb // CHUNK)
    return pl.pallas_call(
        kern,
        grid=(b, s // tb),
        in_specs=[
            pl.BlockSpec((1, tb, HQIG_W), lambda i, j: (i, j, 0)),
            pl.BlockSpec((1, tb, HF_W), lambda i, j: (i, j, 0)),
            pl.BlockSpec((1, HGRN_KEY_WIDTH), lambda i, j: (0, 0)),
            pl.BlockSpec((1, HGRN_KEY_WIDTH), lambda i, j: (0, 0)),
            pl.BlockSpec((1, HGRN_PAD_WIDTH), lambda i, j: (0, 0)),
            pl.BlockSpec(cmat.shape, lambda i, j: (0, 0)),
        ],
        out_specs=pl.BlockSpec((1, tb, HGRN_PAD_WIDTH), lambda i, j: (i, j, 0)),
        out_shape=jax.ShapeDtypeStruct((b, s, HGRN_PAD_WIDTH), bf16),
        scratch_shapes=[pltpu.VMEM((HGRN_HEADS, HGRN_VAL_PAD, HGRN_KEY_DIM), f32)],
        compiler_params=_cparams(("parallel", "arbitrary")),
        name="hgrn",
    )(hqig, hf, llb, l1m, gn, cmat)


_CONV_HIST = 32
_CONV_ROWS = 64


def _conv_kernel(cab_ref, w_ref, b_ref, g_ref, beta_ref, y_ref, ubuf_ref, shift_ref, *, ts):
    @pl.when(pl.program_id(1) == 0)
    def _():
        ubuf_ref[0:_CONV_HIST, :] = jnp.zeros((_CONV_HIST, CONV_CH), f32)

    ca = cab_ref[0, :, 0:CONV_CH].astype(f32)
    cb = cab_ref[0, :, CONV_CH:2 * CONV_CH].astype(f32)
    ubuf_ref[_CONV_HIST:_CONV_HIST + ts, :] = ca * _sigmoid(cb)

    span = ts + _CONV_HIST - SUBLANE
    for r in range(1, SUBLANE):
        shift_ref[r - 1] = ubuf_ref[r:r + span, :]

    first = _CONV_HIST - (CONV_WIDTH - 1)
    for r0 in range(0, ts, _CONV_ROWS):
        acc = jnp.broadcast_to(b_ref[...], (_CONV_ROWS, CONV_CH))
        for j in range(CONV_WIDTH):
            a, r = divmod(first + j, SUBLANE)
            src = ubuf_ref if r == 0 else shift_ref.at[r - 1]
            acc = acc + w_ref[j:j + 1, :] * src[r0 + a * SUBLANE:r0 + a * SUBLANE + _CONV_ROWS, :]
        mu = jnp.mean(acc, axis=-1, keepdims=True)
        xc = acc - mu
        yn = xc * lax.rsqrt(jnp.mean(xc * xc, axis=-1, keepdims=True) + EPS)
        yn = yn * g_ref[...] + beta_ref[...]
        y_ref[0, r0:r0 + _CONV_ROWS, :] = _silu(yn).astype(bf16)

    ubuf_ref[0:_CONV_HIST, :] = ubuf_ref[ts:ts + _CONV_HIST, :]


def _conv(cab, w, bias, g, beta, ts):
    b, s, _ = cab.shape
    kern = functools.partial(_conv_kernel, ts=ts)
    vec = pl.BlockSpec((1, CONV_CH), lambda i, j: (0, 0))
    return pl.pallas_call(
        kern,
        grid=(b, s // ts),
        in_specs=[
            pl.BlockSpec((1, ts, CAB_W), lambda i, j: (i, j, 0)),
            pl.BlockSpec((CONV_WIDTH, CONV_CH), lambda i, j: (0, 0)),
            vec, vec, vec,
        ],
        out_specs=pl.BlockSpec((1, ts, CONV_CH), lambda i, j: (i, j, 0)),
        out_shape=jax.ShapeDtypeStruct((b, s, CONV_CH), bf16),
        scratch_shapes=[pltpu.VMEM((ts + _CONV_HIST, CONV_CH), f32),
                        pltpu.VMEM((SUBLANE - 1, ts + _CONV_HIST - SUBLANE, CONV_CH), f32)],
        compiler_params=_cparams(("parallel", "arbitrary")),
        name="conv",
    )(cab, w, bias, g, beta)


def _merge_kernel(ya_ref, yb_ref, yc_ref, gl_ref, gb_ref, x_ref, wa_ref, wb_ref, wc_ref, wo_ref,
                  gn_ref, xo_ref, h_ref):
    merged = None
    for i, (y_ref, w_ref) in enumerate(((ya_ref, wa_ref), (yb_ref, wb_ref), (yc_ref, wc_ref))):
        cols = slice(i * D_MODEL, (i + 1) * D_MODEL)
        d = _dot(y_ref[...], w_ref[...])
        term = jnp.tanh(gl_ref[:, cols].astype(f32) + gb_ref[:, cols]) * d + d
        merged = term if merged is None else merged + term
    x = x_ref[...] + _dot(merged.astype(bf16), wo_ref[...])
    xo_ref[...] = x
    h_ref[...] = (x * lax.rsqrt(jnp.mean(x * x, axis=-1, keepdims=True) + EPS) * gn_ref[...]).astype(bf16)


def _merge(ya, yb, yc, gl, gate_b, x2d, wa, wb, wc, wo, gn, tm):
    m = x2d.shape[0]
    row = lambda wd: pl.BlockSpec((tm, wd), lambda i: (i, 0))
    full = lambda a: pl.BlockSpec(a.shape, lambda i: (0, 0))
    return pl.pallas_call(
        _merge_kernel,
        grid=(m // tm,),
        in_specs=[row(FOX_WIDTH), row(HGRN_PAD_WIDTH), row(CONV_CH), row(GL_W), full(gate_b),
                  row(D_MODEL), full(wa), full(wb), full(wc), full(wo), full(gn)],
        out_specs=[row(D_MODEL), row(D_MODEL)],
        out_shape=[jax.ShapeDtypeStruct((m, D_MODEL), f32), jax.ShapeDtypeStruct((m, D_MODEL), bf16)],
        compiler_params=_cparams(("parallel",)),
        name="merge",
    )(ya, yb, yc, gl, gate_b, x2d, wa, wb, wc, wo, gn)


_FFN_HALO = 2 * SUBLANE
_FFN_COLS = 256


def _ffn_kernel(h_ref, hprev_ref, x_ref, wup_ref, cw_ref, cb_ref, wd_ref, o_ref, gbuf_ref, act_ref,
                *, tm, blocks_per_seq):
    i = pl.program_id(0)
    seq_start = (i % blocks_per_seq) == 0
    h_ext = jnp.concatenate([hprev_ref[...], h_ref[...]], axis=0)
    row = lax.broadcasted_iota(jnp.int32, (tm + _FFN_HALO, _FFN_COLS), 0)
    keep = jnp.logical_or(row >= _FFN_HALO, jnp.logical_not(seq_start))
    for n, c0 in enumerate(range(0, D_FF, _FFN_COLS)):
        cols = slice(c0, c0 + _FFN_COLS)
        gbuf = gbuf_ref.at[n % 2]
        g_ext = _dot(h_ext, wup_ref[:, cols])
        gbuf[...] = jnp.where(keep, g_ext, 0.0)
        conv = cb_ref[:, cols] + cw_ref[FFN_CONV_WIDTH - 1:FFN_CONV_WIDTH, cols] * g_ext[_FFN_HALO:]
        for k in range(FFN_CONV_WIDTH - 1):
            off = _FFN_HALO - (FFN_CONV_WIDTH - 1) + k
            conv = conv + cw_ref[k:k + 1, cols] * gbuf[off:off + tm, :]
        val = _dot(h_ref[...], wup_ref[:, D_FF + c0:D_FF + c0 + _FFN_COLS])
        act_ref[:, cols] = (_silu(conv) * val).astype(bf16)
    o_ref[...] = x_ref[...] + _dot(act_ref[...], wd_ref[...])


def _ffn(h2, x2d, wup, cw, cb, wd, tm, seq):
    m = x2d.shape[0]
    kern = functools.partial(_ffn_kernel, tm=tm, blocks_per_seq=seq // tm)
    halo_blocks = tm // _FFN_HALO
    resident = lambda a: pl.BlockSpec(a.shape, lambda i: (0, 0), pipeline_mode=pl.Buffered(1))
    return pl.pallas_call(
        kern,
        grid=(m // tm,),
        in_specs=[
            pl.BlockSpec((tm, D_MODEL), lambda i: (i, 0)),
            pl.BlockSpec((_FFN_HALO, D_MODEL), lambda i: (jnp.maximum(i * halo_blocks - 1, 0), 0)),
            pl.BlockSpec((tm, D_MODEL), lambda i: (i, 0)),
            resident(wup), resident(cw), resident(cb), resident(wd),
        ],
        out_specs=pl.BlockSpec((tm, D_MODEL), lambda i: (i, 0)),
        out_shape=jax.ShapeDtypeStruct((m, D_MODEL), f32),
        scratch_shapes=[pltpu.VMEM((2, tm + _FFN_HALO, _FFN_COLS), f32), pltpu.VMEM((tm, D_FF), bf16)],
        compiler_params=_cparams(("parallel",)),
        name="ffn",
    )(h2, h2, x2d, wup, cw, cb, wd)


def _pad_heads(w, axis):
    shape = list(w.shape)
    shape[axis:axis + 1] = [HGRN_HEADS, HGRN_VAL_DIM]
    w = w.reshape(shape)
    pad = [(0, 0)] * len(shape)
    pad[axis + 1] = (0, HGRN_VAL_PAD - HGRN_VAL_DIM)
    w = jnp.pad(w, pad)
    shape[axis:axis + 2] = [HGRN_PAD_WIDTH]
    return w.reshape(shape)


def _block(total, want):
    return min(total, want)


@jax.jit
def _forward(x, norm_mix_g, w_in, fox_forget_b, fox_q_norm_g, fox_k_norm_g, hgrn_lb_logits,
             hgrn_out_norm_g, conv_dw_w, conv_dw_b, conv_norm_g, conv_norm_b, gate_b, w_branch,
             w_out, norm_ffn_g, w_up, ffn_dw_w, ffn_dw_b, w_down):
    b, s, d = x.shape
    depth = w_in.shape[0]
    m = b * s

    tm_in = _block(m, 512)
    ts_prep = _block(s, 512)
    tq = _block(s, 1024)
    tk = 256
    tb_hgrn = _block(s, 512)
    ts_conv = _block(s, 256)
    tm_merge = _block(m, 512)
    tm_ffn = _block(s, 512)

    lb_all = jnp.cumsum(jax.nn.softmax(hgrn_lb_logits.astype(f32), axis=0), axis=0)
    lb_all = jnp.maximum(lb_all - lb_all[0:1], 0.0)
    log_lb = jnp.log(lb_all)
    log_1m_lb = jnp.log1p(-lb_all)

    tri = jnp.asarray(np.tril(np.ones((ts_prep, ts_prep), np.float32)), bf16)
    cmat = jnp.asarray(_hgrn_prefix_matrix(), bf16)
    o_a = FOX_WIDTH
    o_b = FOX_WIDTH + HGRN_VAL_WIDTH

    w_in_cat = _w_in_prep(w_in)
    w_up_bf = w_up.astype(bf16)
    w_down_bf = w_down.astype(bf16)

    x2d = x.reshape(m, d)
    for l in range(depth):
        qkv, ff, hqig, hf, cab, gl = _in_proj(x2d, norm_mix_g[l][None, :], w_in_cat, l, tm_in)

        fb = jnp.pad(fox_forget_b[l], (0, LANE - FOX_HEADS))[None, :]
        gq = jnp.tile(fox_q_norm_g[l], FOX_HEADS)[None, :]
        gk = jnp.tile(fox_k_norm_g[l], FOX_HEADS)[None, :]
        qa, ka, vt = _fox_prep(qkv.reshape(b, s, QKV_W), ff.reshape(b, s, FF_W), fb, gq, gk, tri, ts_prep)
        ya = _fox_attn(qa, ka, vt, tq, tk)

        gn = jnp.tile(jnp.pad(hgrn_out_norm_g[l], (0, HGRN_VAL_PAD - HGRN_VAL_DIM)), HGRN_HEADS)[None, :]
        yb = _hgrn(hqig.reshape(b, s, HQIG_W), hf.reshape(b, s, HF_W), log_lb[l][None, :],
                   log_1m_lb[l][None, :], gn, cmat, tb_hgrn)

        yc = _conv(cab.reshape(b, s, CAB_W), conv_dw_w[l], conv_dw_b[l][None, :],
                   conv_norm_g[l][None, :], conv_norm_b[l][None, :], ts_conv)

        wb = w_branch[l]
        x2d, h2 = _merge(ya.reshape(m, FOX_WIDTH), yb.reshape(m, HGRN_PAD_WIDTH), yc.reshape(m, CONV_CH),
                         gl, 0.5 * gate_b[l][None, :], x2d,
                         wb[:o_a].astype(bf16), _pad_heads(wb[o_a:o_b], 0).astype(bf16),
                         wb[o_b:].astype(bf16), (0.5 * w_out[l]).astype(bf16), norm_ffn_g[l][None, :],
                         tm_merge)

        x2d = _ffn(h2, x2d, w_up_bf[l], ffn_dw_w[l], ffn_dw_b[l][None, :], w_down_bf[l], tm_ffn, s)
    return x2d.reshape(b, s, d)


def kernel(x, norm_mix_g, w_in, fox_forget_b, fox_q_norm_g, fox_k_norm_g, hgrn_lb_logits,
           hgrn_out_norm_g, conv_dw_w, conv_dw_b, conv_norm_g, conv_norm_b, gate_b, w_branch, w_out,
           norm_ffn_g, w_up, ffn_dw_w, ffn_dw_b, w_down):
    return _forward(x, norm_mix_g, w_in, fox_forget_b, fox_q_norm_g, fox_k_norm_g, hgrn_lb_logits,
                    hgrn_out_norm_g, conv_dw_w, conv_dw_b, conv_norm_g, conv_norm_b, gate_b, w_branch,
                    w_out, norm_ffn_g, w_up, ffn_dw_w, ffn_dw_b, w_down)
```

```python
import functools

import numpy as np
import jax
import jax.numpy as jnp
from jax import lax
from jax.experimental import pallas as pl
from jax.experimental.pallas import tpu as pltpu

D_MODEL = 1024
CHUNK = 64
FOX_HEADS = 6
FOX_HEAD_DIM = 64
FOX_WIDTH = FOX_HEADS * FOX_HEAD_DIM
HGRN_HEADS = 4
HGRN_KEY_DIM = 128
HGRN_VAL_DIM = 96
HGRN_KEY_WIDTH = HGRN_HEADS * HGRN_KEY_DIM
HGRN_VAL_WIDTH = HGRN_HEADS * HGRN_VAL_DIM
CONV_CH = 256
CONV_WIDTH = 31
N_BRANCH = 3
D_FF = 2816
FFN_CONV_WIDTH = 3
EPS = 1e-6

LANE = 128
SUBLANE = 8
VMEM_LIMIT_BYTES = 56 * 1024 * 1024

HGRN_VAL_PAD = LANE
HGRN_PAD_WIDTH = HGRN_HEADS * HGRN_VAL_PAD
FOX_AUG = LANE
FOX_VT_ROWS = 80
FOX_PAIRS = FOX_HEADS // 2
NEG_BIG = -1e30
LOG2_E = 1.4426950408889634

QKV_W = 3 * FOX_WIDTH
FF_W = LANE
HQIG_W = HGRN_KEY_WIDTH + 2 * HGRN_PAD_WIDTH
HF_W = HGRN_KEY_WIDTH
CAB_W = 2 * CONV_CH
GL_W = N_BRANCH * D_MODEL
IN_PROJ_W = QKV_W + FF_W + HQIG_W + HF_W + CAB_W + GL_W

f32 = jnp.float32
bf16 = jnp.bfloat16


def _cparams(semantics):
    return pltpu.CompilerParams(dimension_semantics=semantics, vmem_limit_bytes=VMEM_LIMIT_BYTES)


def _log_sigmoid(x):
    return jnp.minimum(x, 0.0) - jnp.log1p(jnp.exp(-jnp.abs(x)))


def _sigmoid(x):
    return 0.5 * jnp.tanh(0.5 * x) + 0.5


def _silu(x):
    u = 0.5 * x
    return u * jnp.tanh(u) + u


def _dot(a, b):
    return jnp.dot(a, b, preferred_element_type=f32)


def _dot_nt(a, b):
    return lax.dot_general(a, b, (((1,), (1,)), ((), ())), preferred_element_type=f32)


def _dot_tn(a, b):
    return lax.dot_general(a, b, (((0,), (0,)), ((), ())), preferred_element_type=f32)


def _split_bf16(x, n):
    parts = []
    r = x
    for _ in range(n):
        p = r.astype(bf16)
        parts.append(p)
        r = r - p.astype(f32)
    return parts


_IN_SIZES = (FOX_WIDTH, FOX_WIDTH, FOX_WIDTH, FOX_HEADS, HGRN_KEY_WIDTH, HGRN_KEY_WIDTH,
             HGRN_VAL_WIDTH, HGRN_VAL_WIDTH, CONV_CH, CONV_CH, N_BRANCH * D_MODEL)
_IN_OFFS = tuple(int(v) for v in np.cumsum((0,) + _IN_SIZES))
_W_PREP_ROWS = 128


def _w_in_prep_kernel(w_ref, o_ref):
    o_fq, _, _, o_ff, o_hq, o_hf, o_hi, o_hg, o_ca, _, o_gl, _ = _IN_OFFS
    rows = w_ref.shape[1]

    def src(a, width):
        return w_ref[0, :, a:a + width]

    def padded_heads(a):
        pieces = []
        for h in range(HGRN_HEADS):
            pieces.append(src(a + h * HGRN_VAL_DIM, HGRN_VAL_DIM))
            pieces.append(jnp.zeros((rows, HGRN_VAL_PAD - HGRN_VAL_DIM), f32))
        return jnp.concatenate(pieces, axis=1)

    segments = [
        src(o_fq, QKV_W),
        jnp.concatenate([src(o_ff, FOX_HEADS), jnp.zeros((rows, FF_W - FOX_HEADS), f32)], axis=1),
        src(o_hq, HGRN_KEY_WIDTH),
        padded_heads(o_hi),
        padded_heads(o_hg),
        src(o_hf, HGRN_KEY_WIDTH),
        src(o_ca, CAB_W),
        src(o_gl, GL_W) * 0.5,
    ]
    c0 = 0
    for seg in segments:
        o_ref[0, :, c0:c0 + seg.shape[1]] = seg.astype(bf16)
        c0 += seg.shape[1]


def _w_in_prep(w_in):
    depth, d, n = w_in.shape
    return pl.pallas_call(
        _w_in_prep_kernel,
        grid=(depth, d // _W_PREP_ROWS),
        in_specs=[pl.BlockSpec((1, _W_PREP_ROWS, n), lambda l, i: (l, i, 0))],
        out_specs=pl.BlockSpec((1, _W_PREP_ROWS, IN_PROJ_W), lambda l, i: (l, i, 0)),
        out_shape=jax.ShapeDtypeStruct((depth, d, IN_PROJ_W), bf16),
        compiler_params=_cparams(("parallel", "parallel")),
        name="w_in_prep",
    )(w_in)


def _in_proj_kernel(x_ref, g_ref, w_ref, qkv_ref, ff_ref, hqig_ref, hf_ref, cab_ref, gl_ref):
    x = x_ref[...]
    h = (x * lax.rsqrt(jnp.mean(x * x, axis=-1, keepdims=True) + EPS) * g_ref[...]).astype(bf16)

    c0 = 0
    z = _dot(h, w_ref[:, c0:c0 + QKV_W + FF_W])
    qkv_ref[...] = z[:, :QKV_W].astype(bf16)
    ff_ref[...] = z[:, QKV_W:]
    c0 += QKV_W + FF_W
    hqig_ref[...] = _dot(h, w_ref[:, c0:c0 + HQIG_W]).astype(bf16)
    c0 += HQIG_W
    hf_ref[...] = _dot(h, w_ref[:, c0:c0 + HF_W])
    c0 += HF_W
    cab_ref[...] = _dot(h, w_ref[:, c0:c0 + CAB_W]).astype(bf16)
    c0 += CAB_W
    half = GL_W // 2
    for a in (0, half):
        gl_ref[:, a:a + half] = _dot(h, w_ref[:, c0 + a:c0 + a + half]).astype(bf16)


def _in_proj(x2d, g, w_all, layer, tm):
    m = x2d.shape[0]
    widths = (QKV_W, FF_W, HQIG_W, HF_W, CAB_W, GL_W)
    dtypes = (bf16, f32, bf16, f32, bf16, bf16)
    return pl.pallas_call(
        _in_proj_kernel,
        grid=(m // tm,),
        in_specs=[
            pl.BlockSpec((tm, D_MODEL), lambda i: (i, 0)),
            pl.BlockSpec((1, D_MODEL), lambda i: (0, 0)),
            pl.BlockSpec((None, D_MODEL, IN_PROJ_W), lambda i: (layer, 0, 0), pipeline_mode=pl.Buffered(1)),
        ],
        out_specs=[pl.BlockSpec((tm, wd), lambda i: (i, 0)) for wd in widths],
        out_shape=[jax.ShapeDtypeStruct((m, wd), dt) for wd, dt in zip(widths, dtypes)],
        compiler_params=_cparams(("parallel",)),
        name="in_proj",
    )(x2d, g, w_all)


def _fox_prep_kernel(qkv_ref, ff_ref, fb_ref, gq_ref, gk_ref, tri_ref,
                     q_out, k_out, vt_out, carry_ref, *, ts):
    @pl.when(pl.program_id(1) == 0)
    def _():
        carry_ref[...] = jnp.zeros_like(carry_ref)

    lane = lax.broadcasted_iota(jnp.int32, (ts, LANE), 1)
    low = lane < FOX_HEAD_DIM

    lf = _log_sigmoid(ff_ref[0] + fb_ref[...])
    parts = jnp.concatenate(_split_bf16(lf, 3), axis=1)
    cs = _dot(tri_ref[...], parts)
    c = (cs[:, :LANE] + carry_ref[...]) + cs[:, LANE:2 * LANE] + cs[:, 2 * LANE:]
    carry_ref[...] = c[ts - 1:ts, :]

    def normed(col0, gain_ref, scale):
        blocks = []
        for j in range(FOX_PAIRS):
            blk = qkv_ref[0, :, col0 + j * LANE:col0 + (j + 1) * LANE].astype(f32)
            sq = blk * blk
            s_lo = jnp.sum(jnp.where(low, sq, 0.0), axis=1, keepdims=True)
            s_hi = jnp.sum(jnp.where(low, 0.0, sq), axis=1, keepdims=True)
            rs = jnp.where(low, lax.rsqrt(s_lo * (1.0 / FOX_HEAD_DIM) + EPS),
                           lax.rsqrt(s_hi * (1.0 / FOX_HEAD_DIM) + EPS))
            blocks.append(blk * rs * (gain_ref[:, j * LANE:(j + 1) * LANE] * scale))
        return blocks

    qn = normed(0, gq_ref, FOX_HEAD_DIM ** -0.5 * LOG2_E)
    kn = normed(FOX_WIDTH, gk_ref, 1.0)
    c2 = c * LOG2_E

    for h in range(FOX_HEADS):
        cb = jnp.broadcast_to(c2[:, h:h + 1], (ts, LANE))
        c3 = [p.astype(f32) for p in _split_bf16(cb, 3)]

        def head_part(blocks):
            blk = blocks[h // 2]
            if h % 2 == 1:
                blk = pltpu.roll(blk, FOX_HEAD_DIM, axis=1)
            return blk

        qa = jnp.where((lane >= FOX_HEAD_DIM + 3) & (lane < FOX_HEAD_DIM + 6), 1.0, 0.0)
        ka = jnp.where((lane >= FOX_HEAD_DIM) & (lane < FOX_HEAD_DIM + 3), 1.0, 0.0)
        for i in range(3):
            qa = jnp.where(lane == FOX_HEAD_DIM + i, c3[i], qa)
            ka = jnp.where(lane == FOX_HEAD_DIM + 3 + i, -c3[i], ka)
        q_out[0, h] = jnp.where(low, head_part(qn), qa).astype(bf16)
        k_out[0, h] = jnp.where(low, head_part(kn), ka).astype(bf16)

    vt = qkv_ref[0, :, 2 * FOX_WIDTH:3 * FOX_WIDTH].astype(f32).T
    tail_row = lax.broadcasted_iota(jnp.int32, (FOX_VT_ROWS - FOX_HEAD_DIM, ts), 0)
    tail = jnp.where(tail_row == 0, 1.0, 0.0).astype(bf16)
    for h in range(FOX_HEADS):
        vt_out[0, h, 0:FOX_HEAD_DIM, :] = vt[h * FOX_HEAD_DIM:(h + 1) * FOX_HEAD_DIM, :].astype(bf16)
        vt_out[0, h, FOX_HEAD_DIM:FOX_VT_ROWS, :] = tail


def _fox_prep(qkv, ff, fb, gq, gk, tri, ts):
    b, s, _ = qkv.shape
    kern = functools.partial(_fox_prep_kernel, ts=ts)
    return pl.pallas_call(
        kern,
        grid=(b, s // ts),
        in_specs=[
            pl.BlockSpec((1, ts, QKV_W), lambda i, j: (i, j, 0)),
            pl.BlockSpec((1, ts, FF_W), lambda i, j: (i, j, 0)),
            pl.BlockSpec((1, LANE), lambda i, j: (0, 0)),
            pl.BlockSpec((1, FOX_WIDTH), lambda i, j: (0, 0)),
            pl.BlockSpec((1, FOX_WIDTH), lambda i, j: (0, 0)),
            pl.BlockSpec((ts, ts), lambda i, j: (0, 0)),
        ],
        out_specs=[
            pl.BlockSpec((1, FOX_HEADS, ts, FOX_AUG), lambda i, j: (i, 0, j, 0)),
            pl.BlockSpec((1, FOX_HEADS, ts, FOX_AUG), lambda i, j: (i, 0, j, 0)),
            pl.BlockSpec((1, FOX_HEADS, FOX_VT_ROWS, ts), lambda i, j: (i, 0, 0, j)),
        ],
        out_shape=[
            jax.ShapeDtypeStruct((b, FOX_HEADS, s, FOX_AUG), bf16),
            jax.ShapeDtypeStruct((b, FOX_HEADS, s, FOX_AUG), bf16),
            jax.ShapeDtypeStruct((b, FOX_HEADS, FOX_VT_ROWS, s), bf16),
        ],
        scratch_shapes=[pltpu.VMEM((1, LANE), f32)],
        compiler_params=_cparams(("parallel", "arbitrary")),
        name="fox_prep",
    )(qkv, ff, fb, gq, gk, tri)


def _fox_attn_kernel(q_ref, k_ref, vt_ref, o_ref, m_ref, acc_ref, s_ref, *, tq, tk):
    qi = pl.program_id(2)
    nsub = tq // tk
    key_idx = lax.broadcasted_iota(jnp.int32, (tk, tk), 0)
    qry_idx = lax.broadcasted_iota(jnp.int32, (tk, tk), 1)
    causal = key_idx <= qry_idx
    m_ref[...] = jnp.full(m_ref.shape, NEG_BIG, f32)
    acc_ref[...] = jnp.zeros(acc_ref.shape, f32)

    def scores(hh, kb, lo):
        start = pl.multiple_of(kb * tk, tk)
        return _dot_nt(k_ref[0, hh, pl.ds(start, tk), :], q_ref[0, hh, lo:tq, :])

    def update(hh, kb, lo, st, diagonal):
        start = pl.multiple_of(kb * tk, tk)
        if diagonal:
            head = jnp.where(causal, st[:, :tk], NEG_BIG)
            st = head if lo + tk == tq else jnp.concatenate([head, st[:, tk:]], axis=1)
        m_old = m_ref[hh, :, lo:tq]
        m_new = jnp.maximum(m_old, jnp.max(st, axis=0, keepdims=True))
        p = jnp.exp2(st - m_new).astype(bf16)
        pv = _dot(vt_ref[0, hh, :, pl.ds(start, tk)], p)
        acc_ref[hh, :, lo:tq] = jnp.exp2(m_old - m_new) * acc_ref[hh, :, lo:tq] + pv
        m_ref[hh, :, lo:tq] = m_new

    n_main = qi * nsub
    for hh in range(2):
        s_ref[0, hh] = scores(hh, 0, 0)

    def body(it, carry):
        kb = it * nsub
        for u in range(nsub):
            slot = u % 2
            for hh in range(2):
                s_ref[1 - slot, hh] = scores(hh, kb + u + 1, 0)
                update(hh, kb + u, 0, s_ref[slot, hh], False)
        return carry

    lax.fori_loop(0, qi, body, 0)
    for e in range(nsub):
        for hh in range(2):
            if e + 1 < nsub:
                s_ref[(e + 1) % 2, hh, :, (e + 1) * tk:tq] = scores(hh, n_main + e + 1, (e + 1) * tk)
            update(hh, n_main + e, e * tk, s_ref[e % 2, hh, :, e * tk:tq], True)

    outs = [acc_ref[hh, 0:FOX_HEAD_DIM, :] / acc_ref[hh, FOX_HEAD_DIM:FOX_HEAD_DIM + 1, :] for hh in range(2)]
    o_ref[0] = jnp.concatenate(outs, axis=0).T.astype(bf16)


def _fox_attn(qa, ka, vt, tq, tk):
    b, _, s, _ = qa.shape
    kern = functools.partial(_fox_attn_kernel, tq=tq, tk=tk)
    return pl.pallas_call(
        kern,
        grid=(b, FOX_PAIRS, s // tq),
        in_specs=[
            pl.BlockSpec((1, 2, tq, FOX_AUG), lambda i, p, j: (i, p, j, 0)),
            pl.BlockSpec((1, 2, s, FOX_AUG), lambda i, p, j: (i, p, 0, 0)),
            pl.BlockSpec((1, 2, FOX_VT_ROWS, s), lambda i, p, j: (i, p, 0, 0)),
        ],
        out_specs=pl.BlockSpec((1, tq, LANE), lambda i, p, j: (i, j, p)),
        out_shape=jax.ShapeDtypeStruct((b, s, FOX_WIDTH), bf16),
        scratch_shapes=[pltpu.VMEM((2, 1, tq), f32), pltpu.VMEM((2, FOX_VT_ROWS, tq), f32),
                        pltpu.VMEM((2, 2, tk, tq), f32)],
        compiler_params=_cparams(("parallel", "parallel", "arbitrary")),
        name="fox_attn",
    )(qa, ka, vt)


_HGRN_HALVES = (1, 2, 4, 8, 16, 32)
_HGRN_MXU_LEVELS = 3


def _hgrn_prefix_matrix():
    mats = []
    for hsz in _HGRN_HALVES[:_HGRN_MXU_LEVELS]:
        m = np.zeros((CHUNK, CHUNK), np.float32)
        for t in range(CHUNK):
            bd = (t // (2 * hsz)) * 2 * hsz + hsz - 1
            if t % (2 * hsz) >= hsz:
                m[t, bd + 1:t + 1] = 1.0
            else:
                m[t, t + 1:bd + 1] = 1.0
        mats.append(m)
    mats.append(np.tril(np.ones((CHUNK, CHUNK), np.float32)))
    return np.concatenate(mats, axis=0)


def _hgrn_kernel(hqig_ref, hf_ref, llb_ref, l1m_ref, gn_ref, cmat_ref, y_ref, st_ref, *, n_chunks):
    @pl.when(pl.program_id(1) == 0)
    def _():
        st_ref[...] = jnp.zeros_like(st_ref)

    t_i = lax.broadcasted_iota(jnp.int32, (CHUNK, CHUNK), 0)
    s_i = lax.broadcasted_iota(jnp.int32, (CHUNK, CHUNK), 1)
    masks = []
    for hsz in _HGRN_HALVES:
        same = (t_i // (2 * hsz)) == (s_i // (2 * hsz))
        masks.append(same & ((t_i % (2 * hsz)) >= hsz) & ((s_i % (2 * hsz)) < hsz))
    cmat = cmat_ref[...]
    kw, vw = HGRN_KEY_WIDTH, HGRN_PAD_WIDTH

    def chunk(c, carry):
        r0 = pl.multiple_of(c * CHUNK, CHUNK)
        rows = pl.ds(r0, CHUNK)
        for h in range(HGRN_HEADS):
            kcols = slice(h * HGRN_KEY_DIM, (h + 1) * HGRN_KEY_DIM)
            q = hqig_ref[0, rows, kcols].astype(f32)
            v = hqig_ref[0, rows, kw + h * HGRN_VAL_PAD:kw + (h + 1) * HGRN_VAL_PAD]
            g = hqig_ref[0, rows, kw + vw + h * HGRN_VAL_PAD:kw + vw + (h + 1) * HGRN_VAL_PAD].astype(f32)
            x = hf_ref[0, rows, kcols]

            ls = jnp.minimum(x, 0.0) - jnp.log(1.0 + jnp.exp(-jnp.abs(x)))
            a = llb_ref[:, kcols]
            bb = l1m_ref[:, kcols] + ls
            logf = jnp.maximum(a, bb) + jnp.log(1.0 + jnp.exp(-jnp.abs(a - bb)))
            kk = jnp.exp(l1m_ref[:, kcols] + ls - x)

            lh = jnp.concatenate(_split_bf16(logf, 2), axis=1)
            e = _dot(cmat, lh)
            e = e[:, :HGRN_KEY_DIM] + e[:, HGRN_KEY_DIM:]
            nl = _HGRN_MXU_LEVELS
            b = e[nl * CHUNK:(nl + 1) * CHUNK]
            decays = [jnp.exp(e[l * CHUNK:(l + 1) * CHUNK]) for l in range(nl)]
            for hsz in _HGRN_HALVES[nl:]:
                bc = jnp.concatenate(
                    [jnp.broadcast_to(b[blk + hsz - 1:blk + hsz, :], (2 * hsz, HGRN_KEY_DIM))
                     for blk in range(0, CHUNK, 2 * hsz)], axis=0)
                decays.append(jnp.exp(-jnp.abs(b - bc)))

            attn = jnp.zeros((CHUNK, CHUNK), f32)
            for mask, dec in zip(masks, decays):
                sc = _dot_nt((q * dec).astype(bf16), (kk * dec).astype(bf16))
                attn = attn + jnp.where(mask, sc, 0.0)
            diag = jnp.sum(q * kk, axis=1, keepdims=True)

            state_t = st_ref[h]
            o = (_dot(attn.astype(bf16), v) + diag * v.astype(f32)
                 + _dot_nt((q * jnp.exp(b)).astype(bf16), state_t.astype(bf16)))
            b_last = b[CHUNK - 1:CHUNK, :]
            kd = (kk * jnp.exp(b_last - b)).astype(bf16)
            st_ref[h] = jnp.exp(b_last) * state_t + _dot_tn(v, kd)

            ms = jnp.sum(o * o, axis=1, keepdims=True) * (1.0 / HGRN_VAL_DIM)
            vcols = slice(h * HGRN_VAL_PAD, (h + 1) * HGRN_VAL_PAD)
            y = o * lax.rsqrt(ms + EPS) * gn_ref[:, vcols] * _silu(g)
            y_ref[0, rows, vcols] = y.astype(bf16)
        return carry

    lax.fori_loop(0, n_chunks, chunk, 0, unroll=8)


def _hgrn(hqig, hf, llb, l1m, gn, cmat, tb):
    b, s, _ = hqig.shape
    kern = functools.partial(_hgrn_kernel, n_chunks=tb // CHUNK)
    return pl.pallas_call(
        kern,
        grid=(b, s // tb),
        in_specs=[
            pl.BlockSpec((1, tb, HQIG_W), lambda i, j: (i, j, 0)),
            pl.BlockSpec((1, tb, HF_W), lambda i, j: (i, j, 0)),
            pl.BlockSpec((1, HGRN_KEY_WIDTH), lambda i, j: (0, 0)),
            pl.BlockSpec((1, HGRN_KEY_WIDTH), lambda i, j: (0, 0)),
            pl.BlockSpec((1, HGRN_PAD_WIDTH), lambda i, j: (0, 0)),
            pl.BlockSpec(cmat.shape, lambda i, j: (0, 0)),
        ],
        out_specs=pl.BlockSpec((1, tb, HGRN_PAD_WIDTH), lambda i, j: (i, j, 0)),
        out_shape=jax.ShapeDtypeStruct((b, s, HGRN_PAD_WIDTH), bf16),
        scratch_shapes=[pltpu.VMEM((HGRN_HEADS, HGRN_VAL_PAD, HGRN_KEY_DIM), f32)],
        compiler_params=_cparams(("parallel", "arbitrary")),
        name="hgrn",
    )(hqig, hf, llb, l1m, gn, cmat)


_CONV_HIST = 32
_CONV_ROWS = 64


def _conv_kernel(cab_ref, w_ref, b_ref, g_ref, beta_ref, y_ref, ubuf_ref, shift_ref, *, ts):
    @pl.when(pl.program_id(1) == 0)
    def _():
        ubuf_ref[0:_CONV_HIST, :] = jnp.zeros((_CONV_HIST, CONV_CH), f32)

    ca = cab_ref[0, :, 0:CONV_CH].astype(f32)
    cb = cab_ref[0, :, CONV_CH:2 * CONV_CH].astype(f32)
    ubuf_ref[_CONV_HIST:_CONV_HIST + ts, :] = ca * _sigmoid(cb)

    span = ts + _CONV_HIST - SUBLANE
    for r in range(1, SUBLANE):
        shift_ref[r - 1] = ubuf_ref[r:r + span, :]

    first = _CONV_HIST - (CONV_WIDTH - 1)
    for r0 in range(0, ts, _CONV_ROWS):
        acc = jnp.broadcast_to(b_ref[...], (_CONV_ROWS, CONV_CH))
        for j in range(CONV_WIDTH):
            a, r = divmod(first + j, SUBLANE)
            src = ubuf_ref if r == 0 else shift_ref.at[r - 1]
            acc = acc + w_ref[j:j + 1, :] * src[r0 + a * SUBLANE:r0 + a * SUBLANE + _CONV_ROWS, :]
        mu = jnp.mean(acc, axis=-1, keepdims=True)
        xc = acc - mu
        yn = xc * lax.rsqrt(jnp.mean(xc * xc, axis=-1, keepdims=True) + EPS)
        yn = yn * g_ref[...] + beta_ref[...]
        y_ref[0, r0:r0 + _CONV_ROWS, :] = _silu(yn).astype(bf16)

    ubuf_ref[0:_CONV_HIST, :] = ubuf_ref[ts:ts + _CONV_HIST, :]


def _conv(cab, w, bias, g, beta, ts):
    b, s, _ = cab.shape
    kern = functools.partial(_conv_kernel, ts=ts)
    vec = pl.BlockSpec((1, CONV_CH), lambda i, j: (0, 0))
    return pl.pallas_call(
        kern,
        grid=(b, s // ts),
        in_specs=[
            pl.BlockSpec((1, ts, CAB_W), lambda i, j: (i, j, 0)),
            pl.BlockSpec((CONV_WIDTH, CONV_CH), lambda i, j: (0, 0)),
            vec, vec, vec,
        ],
        out_specs=pl.BlockSpec((1, ts, CONV_CH), lambda i, j: (i, j, 0)),
        out_shape=jax.ShapeDtypeStruct((b, s, CONV_CH), bf16),
        scratch_shapes=[pltpu.VMEM((ts + _CONV_HIST, CONV_CH), f32),
                        pltpu.VMEM((SUBLANE - 1, ts + _CONV_HIST - SUBLANE, CONV_CH), f32)],
        compiler_params=_cparams(("parallel", "arbitrary")),
        name="conv",
    )(cab, w, bias, g, beta)


def _merge_kernel(ya_ref, yb_ref, yc_ref, gl_ref, gb_ref, x_ref, wa_ref, wb_ref, wc_ref, wo_ref,
                  gn_ref, xo_ref, h_ref):
    merged = None
    for i, (y_ref, w_ref) in enumerate(((ya_ref, wa_ref), (yb_ref, wb_ref), (yc_ref, wc_ref))):
        cols = slice(i * D_MODEL, (i + 1) * D_MODEL)
        d = _dot(y_ref[...], w_ref[...])
        term = jnp.tanh(gl_ref[:, cols].astype(f32) + gb_ref[:, cols]) * d + d
        merged = term if merged is None else merged + term
    x = x_ref[...] + _dot(merged.astype(bf16), wo_ref[...])
    xo_ref[...] = x
    h_ref[...] = (x * lax.rsqrt(jnp.mean(x * x, axis=-1, keepdims=True) + EPS) * gn_ref[...]).astype(bf16)


def _merge(ya, yb, yc, gl, gate_b, x2d, wa, wb, wc, wo, gn, tm):
    m = x2d.shape[0]
    row = lambda wd: pl.BlockSpec((tm, wd), lambda i: (i, 0))
    full = lambda a: pl.BlockSpec(a.shape, lambda i: (0, 0))
    return pl.pallas_call(
        _merge_kernel,
        grid=(m // tm,),
        in_specs=[row(FOX_WIDTH), row(HGRN_PAD_WIDTH), row(CONV_CH), row(GL_W), full(gate_b),
                  row(D_MODEL), full(wa), full(wb), full(wc), full(wo), full(gn)],
        out_specs=[row(D_MODEL), row(D_MODEL)],
        out_shape=[jax.ShapeDtypeStruct((m, D_MODEL), f32), jax.ShapeDtypeStruct((m, D_MODEL), bf16)],
        compiler_params=_cparams(("parallel",)),
        name="merge",
    )(ya, yb, yc, gl, gate_b, x2d, wa, wb, wc, wo, gn)


_FFN_HALO = 2 * SUBLANE
_FFN_COLS = 256


def _ffn_kernel(h_ref, hprev_ref, x_ref, wup_ref, cw_ref, cb_ref, wd_ref, o_ref, gbuf_ref, act_ref,
                *, tm, blocks_per_seq):
    i = pl.program_id(0)
    seq_start = (i % blocks_per_seq) == 0
    h_ext = jnp.concatenate([hprev_ref[...], h_ref[...]], axis=0)
    row = lax.broadcasted_iota(jnp.int32, (tm + _FFN_HALO, _FFN_COLS), 0)
    keep = jnp.logical_or(row >= _FFN_HALO, jnp.logical_not(seq_start))
    for n, c0 in enumerate(range(0, D_FF, _FFN_COLS)):
        cols = slice(c0, c0 + _FFN_COLS)
        gbuf = gbuf_ref.at[n % 2]
        g_ext = _dot(h_ext, wup_ref[:, cols])
        gbuf[...] = jnp.where(keep, g_ext, 0.0)
        conv = cb_ref[:, cols] + cw_ref[FFN_CONV_WIDTH - 1:FFN_CONV_WIDTH, cols] * g_ext[_FFN_HALO:]
        for k in range(FFN_CONV_WIDTH - 1):
            off = _FFN_HALO - (FFN_CONV_WIDTH - 1) + k
            conv = conv + cw_ref[k:k + 1, cols] * gbuf[off:off + tm, :]
        val = _dot(h_ref[...], wup_ref[:, D_FF + c0:D_FF + c0 + _FFN_COLS])
        act_ref[:, cols] = (conv * _sigmoid(conv) * val).astype(bf16)
    o_ref[...] = x_ref[...] + _dot(act_ref[...], wd_ref[...])


def _ffn(h2, x2d, wup, cw, cb, wd, tm, seq):
    m = x2d.shape[0]
    kern = functools.partial(_ffn_kernel, tm=tm, blocks_per_seq=seq // tm)
    halo_blocks = tm // _FFN_HALO
    resident = lambda a: pl.BlockSpec(a.shape, lambda i: (0, 0), pipeline_mode=pl.Buffered(1))
    return pl.pallas_call(
        kern,
        grid=(m // tm,),
        in_specs=[
            pl.BlockSpec((tm, D_MODEL), lambda i: (i, 0)),
            pl.BlockSpec((_FFN_HALO, D_MODEL), lambda i: (jnp.maximum(i * halo_blocks - 1, 0), 0)),
            pl.BlockSpec((tm, D_MODEL), lambda i: (i, 0)),
            resident(wup), resident(cw), resident(cb), resident(wd),
        ],
        out_specs=pl.BlockSpec((tm, D_MODEL), lambda i: (i, 0)),
        out_shape=jax.ShapeDtypeStruct((m, D_MODEL), f32),
        scratch_shapes=[pltpu.VMEM((2, tm + _FFN_HALO, _FFN_COLS), f32), pltpu.VMEM((tm, D_FF), bf16)],
        compiler_params=_cparams(("parallel",)),
        name="ffn",
    )(h2, h2, x2d, wup, cw, cb, wd)


def _pad_heads(w, axis):
    shape = list(w.shape)
    shape[axis:axis + 1] = [HGRN_HEADS, HGRN_VAL_DIM]
    w = w.reshape(shape)
    pad = [(0, 0)] * len(shape)
    pad[axis + 1] = (0, HGRN_VAL_PAD - HGRN_VAL_DIM)
    w = jnp.pad(w, pad)
    shape[axis:axis + 2] = [HGRN_PAD_WIDTH]
    return w.reshape(shape)


def _block(total, want):
    return min(total, want)


@jax.jit
def _forward(x, norm_mix_g, w_in, fox_forget_b, fox_q_norm_g, fox_k_norm_g, hgrn_lb_logits,
             hgrn_out_norm_g, conv_dw_w, conv_dw_b, conv_norm_g, conv_norm_b, gate_b, w_branch,
             w_out, norm_ffn_g, w_up, ffn_dw_w, ffn_dw_b, w_down):
    b, s, d = x.shape
    depth = w_in.shape[0]
    m = b * s

    tm_in = _block(m, 512)
    ts_prep = _block(s, 512)
    tq = _block(s, 1024)
    tk = 256
    tb_hgrn = _block(s, 512)
    ts_conv = _block(s, 256)
    tm_merge = _block(m, 512)
    tm_ffn = _block(s, 512)

    lb_all = jnp.cumsum(jax.nn.softmax(hgrn_lb_logits.astype(f32), axis=0), axis=0)
    lb_all = jnp.maximum(lb_all - lb_all[0:1], 0.0)
    log_lb = jnp.log(lb_all)
    log_1m_lb = jnp.log1p(-lb_all)

    tri = jnp.asarray(np.tril(np.ones((ts_prep, ts_prep), np.float32)), bf16)
    cmat = jnp.asarray(_hgrn_prefix_matrix(), bf16)
    o_a = FOX_WIDTH
    o_b = FOX_WIDTH + HGRN_VAL_WIDTH

    w_in_cat = _w_in_prep(w_in)
    w_up_bf = w_up.astype(bf16)
    w_down_bf = w_down.astype(bf16)

    x2d = x.reshape(m, d)
    for l in range(depth):
        qkv, ff, hqig, hf, cab, gl = _in_proj(x2d, norm_mix_g[l][None, :], w_in_cat, l, tm_in)

        fb = jnp.pad(fox_forget_b[l], (0, LANE - FOX_HEADS))[None, :]
        gq = jnp.tile(fox_q_norm_g[l], FOX_HEADS)[None, :]
        gk = jnp.tile(fox_k_norm_g[l], FOX_HEADS)[None, :]
        qa, ka, vt = _fox_prep(qkv.reshape(b, s, QKV_W), ff.reshape(b, s, FF_W), fb, gq, gk, tri, ts_prep)
        ya = _fox_attn(qa, ka, vt, tq, tk)

        gn = jnp.tile(jnp.pad(hgrn_out_norm_g[l], (0, HGRN_VAL_PAD - HGRN_VAL_DIM)), HGRN_HEADS)[None, :]
        yb = _hgrn(hqig.reshape(b, s, HQIG_W), hf.reshape(b, s, HF_W), log_lb[l][None, :],
                   log_1m_lb[l][None, :], gn, cmat, tb_hgrn)

        yc = _conv(cab.reshape(b, s, CAB_W), conv_dw_w[l], conv_dw_b[l][None, :],
                   conv_norm_g[l][None, :], conv_norm_b[l][None, :], ts_conv)

        wb = w_branch[l]
        x2d, h2 = _merge(ya.reshape(m, FOX_WIDTH), yb.reshape(m, HGRN_PAD_WIDTH), yc.reshape(m, CONV_CH),
                         gl, 0.5 * gate_b[l][None, :], x2d,
                         wb[:o_a].astype(bf16), _pad_heads(wb[o_a:o_b], 0).astype(bf16),
                         wb[o_b:].astype(bf16), (0.5 * w_out[l]).astype(bf16), norm_ffn_g[l][None, :],
                         tm_merge)

        x2d = _ffn(h2, x2d, w_up_bf[l], ffn_dw_w[l], ffn_dw_b[l][None, :], w_down_bf[l], tm_ffn, s)
    return x2d.reshape(b, s, d)


def kernel(x, norm_mix_g, w_in, fox_forget_b, fox_q_norm_g, fox_k_norm_g, hgrn_lb_logits,
           hgrn_out_norm_g, conv_dw_w, conv_dw_b, conv_norm_g, conv_norm_b, gate_b, w_branch, w_out,
           norm_ffn_g, w_up, ffn_dw_w, ffn_dw_b, w_down):
    return _forward(x, norm_mix_g, w_in, fox_forget_b, fox_q_norm_g, fox_k_norm_g, hgrn_lb_logits,
                    hgrn_out_norm_g, conv_dw_w, conv_dw_b, conv_norm_g, conv_norm_b, gate_b, w_branch,
                    w_out, norm_ffn_g, w_up, ffn_dw_w, ffn_dw_b, w_down)
```
